```python
import math
import jax
import jax.numpy as jnp
from jax import lax
import numpy as np

D_MODEL = 1024
BATCH = 8
SEQ = 2048
DEPTH = 1
DEC_BATCH = 128
DEC_SEQ = 1
PAST_LEN = 16384
PAGE_SIZE = 128

POOL_WIDTH = D_MODEL // 2
POOL_WINDOWS = (2, 4, 8, 16)
POOL_GROUPS = len(POOL_WINDOWS)
POOL_GROUP_W = POOL_WIDTH // POOL_GROUPS
POOL_BUF = max(POOL_WINDOWS) - 1
SSM_WIDTH = D_MODEL // 2
SSM_GROUP_W = 16
SSM_GROUPS = SSM_WIDTH // SSM_GROUP_W
SSM_STATE = 64
DT_MIN = 1e-3
DT_MAX = 1e-1
D_FF = 4 * D_MODEL
RMS_EPS = 1e-6
IN_WIDTH = POOL_WIDTH + SSM_WIDTH + 2 * D_MODEL

kernel_name = 'hybrid_pool_s5_gated_decoder_step'


def rmsnorm(x, g):
    xf = x.astype(jnp.float32)
    y = xf * lax.rsqrt(jnp.mean(xf * xf, axis=-1, keepdims=True) + RMS_EPS)
    return (y * g.astype(jnp.float32)).astype(x.dtype)


def pool_mixer(v, buf, start_pos, w_grp, scale):
    bsz, seq_len, _ = v.shape
    vf = v.astype(jnp.float32)
    cat = jnp.concatenate([buf.astype(jnp.float32), vf], axis=1)
    csum = jnp.cumsum(cat, axis=1)
    csum = jnp.concatenate([jnp.zeros_like(csum[:, :1]), csum], axis=1)
    end = csum[:, POOL_BUF + 1:]
    pos = start_pos + jnp.arange(seq_len) + 1
    outs = []
    for g, w in enumerate(POOL_WINDOWS):
        sl = slice(g * POOL_GROUP_W, (g + 1) * POOL_GROUP_W)
        win_sum = end[..., sl] - csum[:, POOL_BUF + 1 - w:POOL_BUF + 1 - w + seq_len, sl]
        cnt = jnp.minimum(w, pos).astype(jnp.float32)[None, :, None]
        outs.append(win_sum / cnt - vf[..., sl])
    pooled = jnp.stack(outs, axis=2)
    mixed = jnp.einsum('blgc,gcd->blgd', pooled, w_grp.astype(jnp.float32))
    y = mixed.reshape(bsz, seq_len, POOL_WIDTH) * scale.astype(jnp.float32)
    new_buf = cat[:, -POOL_BUF:]
    return y.astype(v.dtype), new_buf.astype(buf.dtype)


def ssm_discretize(a_re, a_im, log_dt, b_re, b_im):
    a_re = a_re.astype(jnp.float32)
    a_im = a_im.astype(jnp.float32)
    b_re = b_re.astype(jnp.float32)
    b_im = b_im.astype(jnp.float32)
    dt = jnp.exp(log_dt.astype(jnp.float32))[:, None]
    mag = jnp.exp(a_re * dt)
    ang = a_im * dt
    lam_re = mag * jnp.cos(ang)
    lam_im = mag * jnp.sin(ang)
    p = lam_re - 1.0
    q = lam_im
    den = a_re * a_re + a_im * a_im
    coef_re = (p * a_re + q * a_im) / den
    coef_im = (q * a_re - p * a_im) / den
    bbar_re = coef_re[..., None] * b_re - coef_im[..., None] * b_im
    bbar_im = coef_re[..., None] * b_im + coef_im[..., None] * b_re
    return lam_re, lam_im, bbar_re, bbar_im


def _ssm_combine(e1, e2):
    a1r, a1i, b1r, b1i = e1
    a2r, a2i, b2r, b2i = e2
    return (a2r * a1r - a2i * a1i,
            a2r * a1i + a2i * a1r,
            a2r * b1r - a2i * b1i + b2r,
            a2r * b1i + a2i * b1r + b2i)


def ssm_mixer(u, h0_re, h0_im, a_re, a_im, log_dt, b_re, b_im, c_re, c_im, d_skip):
    bsz, seq_len, _ = u.shape
    uf = u.astype(jnp.float32).reshape(bsz, seq_len, SSM_GROUPS, SSM_GROUP_W)
    lam_re, lam_im, bbar_re, bbar_im = ssm_discretize(a_re, a_im, log_dt, b_re, b_im)
    bu_re = jnp.einsum('blgc,gnc->blgn', uf, bbar_re)
    bu_im = jnp.einsum('blgc,gnc->blgn', uf, bbar_im)
    h0r = h0_re.astype(jnp.float32)
    h0i = h0_im.astype(jnp.float32)
    bu_re = bu_re.at[:, 0].add(lam_re * h0r - lam_im * h0i)
    bu_im = bu_im.at[:, 0].add(lam_re * h0i + lam_im * h0r)
    ar = jnp.broadcast_to(lam_re, bu_re.shape)
    ai = jnp.broadcast_to(lam_im, bu_im.shape)
    _, _, h_re, h_im = lax.associative_scan(_ssm_combine, (ar, ai, bu_re, bu_im), axis=1)
    y = (jnp.einsum('blgn,gcn->blgc', h_re, c_re.astype(jnp.float32))
         - jnp.einsum('blgn,gcn->blgc', h_im, c_im.astype(jnp.float32))
         + d_skip.astype(jnp.float32).reshape(SSM_GROUPS, SSM_GROUP_W) * uf)
    y = y.reshape(bsz, seq_len, SSM_WIDTH)
    return y.astype(u.dtype), h_re[:, -1].astype(h0_re.dtype), h_im[:, -1].astype(h0_im.dtype)


def hybrid_layer(x, pool_buf, h_re, h_im, start_pos, norm1_g, w_in, pool_w, pool_scale, pool_out,
                 ssm_a_re, ssm_a_im, ssm_log_dt, ssm_b_re, ssm_b_im, ssm_c_re, ssm_c_im, ssm_d,
                 ssm_glu, w_out, norm2_g, w_up, w_down):
    h = rmsnorm(x, norm1_g)
    proj = h @ w_in
    v = proj[..., :POOL_WIDTH]
    u = proj[..., POOL_WIDTH:POOL_WIDTH + SSM_WIDTH]
    gates = jax.nn.sigmoid(proj[..., POOL_WIDTH + SSM_WIDTH:].astype(jnp.float32))
    g_pool = gates[..., :D_MODEL]
    g_ssm = gates[..., D_MODEL:]
    yp, new_buf = pool_mixer(v, pool_buf, start_pos, pool_w, pool_scale)
    branch_pool = (yp @ pool_out).astype(jnp.float32)
    ys, new_re, new_im = ssm_mixer(u, h_re, h_im, ssm_a_re, ssm_a_im, ssm_log_dt,
                                   ssm_b_re, ssm_b_im, ssm_c_re, ssm_c_im, ssm_d)
    z = (ys @ ssm_glu).astype(jnp.float32)
    branch_ssm = z[..., :D_MODEL] * jax.nn.sigmoid(z[..., D_MODEL:])
    merged = (g_pool * branch_pool + g_ssm * branch_ssm).astype(x.dtype)
    x = x + merged @ w_out
    h2 = rmsnorm(x, norm2_g)
    x = x + jnp.square(jax.nn.relu(h2 @ w_up)) @ w_down
    return x, new_buf, new_re, new_im


def setup_inputs(seed: int = 0) -> dict:
    key = jax.random.key(seed)
    ks = jax.random.split(key, 24)
    f32 = jnp.float32
    nrm = lambda k, shape, s: jax.random.normal(k, shape, f32) * s
    a_im_base = jnp.broadcast_to(math.pi * jnp.arange(SSM_STATE, dtype=f32), (DEPTH, SSM_GROUPS, SSM_STATE))
    return {
        'x_prompt': nrm(ks[0], (BATCH, SEQ, D_MODEL), 1.0),
        'x_sample': nrm(ks[1], (DEC_BATCH, DEC_SEQ, D_MODEL), 1.0),
        'cache_pool': nrm(ks[2], (DEPTH, DEC_BATCH, POOL_BUF, POOL_WIDTH), 1.0),
        'state_ssm_re': nrm(ks[3], (DEPTH, DEC_BATCH, SSM_GROUPS, SSM_STATE), 0.3),
        'state_ssm_im': nrm(ks[4], (DEPTH, DEC_BATCH, SSM_GROUPS, SSM_STATE), 0.3),
        'norm1_g': 1.0 + nrm(ks[5], (DEPTH, D_MODEL), 0.02),
        'w_in': nrm(ks[6], (DEPTH, D_MODEL, IN_WIDTH), D_MODEL ** -0.5),
        'pool_w': nrm(ks[7], (DEPTH, POOL_GROUPS, POOL_GROUP_W, POOL_GROUP_W), POOL_GROUP_W ** -0.5),
        'pool_scale': 1.0 + nrm(ks[8], (DEPTH, POOL_WIDTH), 0.1),
        'pool_out': nrm(ks[9], (DEPTH, POOL_WIDTH, D_MODEL), POOL_WIDTH ** -0.5),
        'ssm_a_re': -0.5 + nrm(ks[10], (DEPTH, SSM_GROUPS, SSM_STATE), 0.01),
        'ssm_a_im': a_im_base + nrm(ks[11], (DEPTH, SSM_GROUPS, SSM_STATE), 0.01),
        'ssm_log_dt': jax.random.uniform(ks[12], (DEPTH, SSM_GROUPS), f32, math.log(DT_MIN), math.log(DT_MAX)),
        'ssm_b_re': nrm(ks[13], (DEPTH, SSM_GROUPS, SSM_STATE, SSM_GROUP_W), (2 * SSM_GROUP_W) ** -0.5),
        'ssm_b_im': nrm(ks[14], (DEPTH, SSM_GROUPS, SSM_STATE, SSM_GROUP_W), (2 * SSM_GROUP_W) ** -0.5),
        'ssm_c_re': nrm(ks[15], (DEPTH, SSM_GROUPS, SSM_GROUP_W, SSM_STATE), (2 * SSM_STATE) ** -0.5),
        'ssm_c_im': nrm(ks[16], (DEPTH, SSM_GROUPS, SSM_GROUP_W, SSM_STATE), (2 * SSM_STATE) ** -0.5),
        'ssm_d': nrm(ks[17], (DEPTH, SSM_WIDTH), 1.0),
        'ssm_glu': nrm(ks[18], (DEPTH, SSM_WIDTH, 2 * D_MODEL), SSM_WIDTH ** -0.5),
        'w_out': nrm(ks[19], (DEPTH, D_MODEL, D_MODEL), D_MODEL ** -0.5),
        'norm2_g': 1.0 + nrm(ks[20], (DEPTH, D_MODEL), 0.02),
        'w_up': nrm(ks[21], (DEPTH, D_MODEL, D_FF), D_MODEL ** -0.5),
        'w_down': nrm(ks[22], (DEPTH, D_FF, D_MODEL), D_FF ** -0.5),
        'normf_g': 1.0 + nrm(ks[23], (D_MODEL,), 0.02),
    }


def reference(x_prompt, x_sample, cache_pool, state_ssm_re, state_ssm_im, norm1_g, w_in, pool_w,
              pool_scale, pool_out, ssm_a_re, ssm_a_im, ssm_log_dt, ssm_b_re, ssm_b_im, ssm_c_re,
              ssm_c_im, ssm_d, ssm_glu, w_out, norm2_g, w_up, w_down, normf_g):
    xp = x_prompt
    xs = x_sample
    pool_p, re_p, im_p, pool_s, re_s, im_s = [], [], [], [], [], []
    for l in range(DEPTH):
        w = (norm1_g[l], w_in[l], pool_w[l], pool_scale[l], pool_out[l], ssm_a_re[l], ssm_a_im[l],
             ssm_log_dt[l], ssm_b_re[l], ssm_b_im[l], ssm_c_re[l], ssm_c_im[l], ssm_d[l], ssm_glu[l],
             w_out[l], norm2_g[l], w_up[l], w_down[l])
        zero_buf = jnp.zeros((xp.shape[0], POOL_BUF, POOL_WIDTH), cache_pool.dtype)
        zero_h = jnp.zeros((xp.shape[0], SSM_GROUPS, SSM_STATE), state_ssm_re.dtype)
        xp, bp, rp, ip = hybrid_layer(xp, zero_buf, zero_h, zero_h, 0, *w)
        xs, bs, rs, is_ = hybrid_layer(xs, cache_pool[l], state_ssm_re[l], state_ssm_im[l], PAST_LEN, *w)
        pool_p.append(bp)
        re_p.append(rp)
        im_p.append(ip)
        pool_s.append(bs)
        re_s.append(rs)
        im_s.append(is_)
    y_prompt = rmsnorm(xp, normf_g)
    y_sample = rmsnorm(xs, normf_g)
    return (y_prompt, y_sample,
            jnp.stack(pool_p), jnp.stack(re_p), jnp.stack(im_p),
            jnp.stack(pool_s), jnp.stack(re_s), jnp.stack(im_s))
```

```python
import functools
import math

import jax
import jax.numpy as jnp
from jax import lax
from jax.experimental import pallas as pl
from jax.experimental.pallas import tpu as pltpu

POOL_WINDOWS = (2, 4, 8, 16)
POOL_BUF = max(POOL_WINDOWS) - 1
SSM_GROUP_W = 16
SSM_STATE = 64
RMS_EPS = 1e-6
PAST_LEN = 16384

LANES = 128
SUBLANES = 8
GROUPS_PER_SLAB = LANES // SSM_GROUP_W
SLAB_STATES = GROUPS_PER_SLAB * SSM_STATE
STEPS_PER_TILE = 32
HALO_STEPS = 16
FF_CHUNK = 1024
VMEM_LIMIT_BYTES = 56 * 1024 * 1024

F32 = jnp.float32
BF16 = jnp.bfloat16


def _dot(a, b):
    return jnp.dot(a, b, preferred_element_type=F32)


def _sigmoid(x):
    return 1.0 / (1.0 + jnp.exp(-x))


def _rmsnorm(x, g):
    y = x * lax.rsqrt(jnp.mean(x * x, axis=-1, keepdims=True) + RMS_EPS)
    return y * g


def _discretize(a_re, a_im, dt):
    mag = jnp.exp(a_re * dt)
    ang = a_im * dt
    lam_re = mag * jnp.cos(ang)
    lam_im = mag * jnp.sin(ang)
    p = lam_re - 1.0
    q = lam_im
    den = a_re * a_re + a_im * a_im
    coef_re = (p * a_re + q * a_im) / den
    coef_im = (q * a_re - p * a_im) / den
    return lam_re, lam_im, coef_re, coef_im


def _ssm_prep_kernel(a_re_ref, a_im_ref, a_re_rep_ref, a_im_rep_ref, log_dt_ref,
                     b_re_ref, b_im_ref,
                     lam_re_ref, lam_im_ref, bbar_re_ref, bbar_im_ref):
    dt = jnp.exp(log_dt_ref[...])
    lam_re, lam_im, _, _ = _discretize(a_re_ref[...], a_im_ref[...], dt)
    lam_re_ref[...] = lam_re
    lam_im_ref[...] = lam_im
    _, _, coef_re, coef_im = _discretize(a_re_rep_ref[...], a_im_rep_ref[...], dt)
    b_re = b_re_ref[...]
    b_im = b_im_ref[...]
    bbar_re_ref[...] = coef_re * b_re - coef_im * b_im
    bbar_im_ref[...] = coef_re * b_im + coef_im * b_re


def _ssm_prep(a_re, a_im, log_dt, b_re, b_im):
    g, n = a_re.shape
    c = b_re.shape[-1]
    full = lambda shape: pl.BlockSpec(shape, lambda i: (0,) * len(shape))
    ins = (a_re, a_im, jnp.repeat(a_re, c, axis=1), jnp.repeat(a_im, c, axis=1),
           log_dt.reshape(g, 1), b_re.reshape(g, n * c), b_im.reshape(g, n * c))
    outs = (jax.ShapeDtypeStruct((g, n), F32), jax.ShapeDtypeStruct((g, n), F32),
            jax.ShapeDtypeStruct((g, n * c), F32), jax.ShapeDtypeStruct((g, n * c), F32))
    lam_re, lam_im, bbar_re, bbar_im = pl.pallas_call(
        _ssm_prep_kernel,
        out_shape=outs,
        grid=(1,),
        in_specs=[full(x.shape) for x in ins],
        out_specs=[full(o.shape) for o in outs],
        name="ssm_prep",
    )(*ins)
    return lam_re, lam_im, bbar_re.reshape(g, n, c), bbar_im.reshape(g, n, c)


def _block_diag_ssm(bbar_re, bbar_im, c_re, c_im):
    g, n, c = bbar_re.shape
    slabs = g // GROUPS_PER_SLAB
    eye = jnp.eye(GROUPS_PER_SLAB, dtype=F32)

    def b_blocks(b):
        b = b.reshape(slabs, GROUPS_PER_SLAB, n, c)
        return jnp.einsum('sgnc,gh->sgchn', b, eye).reshape(slabs, LANES, SLAB_STATES)

    def c_blocks(m):
        m = m.reshape(slabs, GROUPS_PER_SLAB, c, n)
        return jnp.einsum('sgcn,gh->sgnhc', m, eye).reshape(slabs, SLAB_STATES, LANES)

    b_cat = jnp.concatenate([b_blocks(bbar_re), b_blocks(bbar_im)], axis=2)
    c_cat = jnp.concatenate([c_blocks(c_re), -c_blocks(c_im)], axis=1)
    return b_cat.astype(BF16), c_cat.astype(BF16)


def _pool_project(pooled, pool_w_ref, pool_scale_ref, pool_out_ref):
    mixed = [_dot(p.astype(BF16), pool_w_ref[g]) for g, p in enumerate(pooled)]
    yp = jnp.concatenate(mixed, axis=-1) * pool_scale_ref[...]
    return _dot(yp.astype(BF16), pool_out_ref[...])


def _finish_layer(x, h1, branch_pool, ys, w_in_ref, glu_ref, w_out_ref, g2_ref,
                  w_up_ref, w_down_ref, gf_ref):
    d = x.shape[-1]
    z =_dot(ys.astype(BF16), glu_ref[...])
    branch_ssm = z[:, :d] * _sigmoid(z[:, d:])
    gate_off = w_in_ref.shape[1] - 2 * d
    g_pool = _sigmoid(_dot(h1, w_in_ref[:, gate_off:gate_off + d]))
    g_ssm = _sigmoid(_dot(h1, w_in_ref[:, gate_off + d:]))
    merged = g_pool * branch_pool + g_ssm * branch_ssm
    x1 = x + _dot(merged.astype(BF16), w_out_ref[...])
    h2 = _rmsnorm(x1, g2_ref[...]).astype(BF16)
    d_ff = w_up_ref.shape[1]
    acc = None
    for c0 in range(0, d_ff, FF_CHUNK):
        a = jnp.maximum(_dot(h2, w_up_ref[:, c0:c0 + FF_CHUNK]), 0.0)
        part = _dot((a * a).astype(BF16), w_down_ref[c0:c0 + FF_CHUNK, :])
        acc = part if acc is None else acc + part
    x2 = x1 + acc
    return _rmsnorm(x2, gf_ref[...])


def _prompt_kernel(x_ref, g1_ref, w_in_ref, pool_w_ref, pool_scale_ref, pool_out_ref,
                   lam_re_ref, lam_im_ref, b_cat_ref, c_cat_ref, d_skip_ref, glu_ref,
                   w_out_ref, g2_ref, w_up_ref, w_down_ref, gf_ref,
                   y_ref, pool_tail_ref, h_fin_ref,
                   vbuf, s_buf, h_state):
    i = pl.program_id(0)
    rows = x_ref.shape[0]
    batch = h_state.shape[0]
    steps = rows // batch
    halo = HALO_STEPS * batch
    pool_w_lanes = LANES
    n_slabs = b_cat_ref.shape[0]

    @pl.when(i == 0)
    def _():
        vbuf[0:halo, :] = jnp.zeros((halo, vbuf.shape[1]), F32)
        h_state[...] = jnp.zeros(h_state.shape, F32)

    x = x_ref[...]
    h1 = _rmsnorm(x, g1_ref[...]).astype(BF16)
    pool_width = vbuf.shape[1]
    v = _dot(h1, w_in_ref[:, 0:pool_width])
    vbuf[halo:halo + rows, :] = v
    u = _dot(h1, w_in_ref[:, pool_width:2 * pool_width])

    row = lax.broadcasted_iota(jnp.int32, (rows, pool_w_lanes), 0)
    pos = i * steps + lax.shift_right_logical(row, int(math.log2(batch))) + 1
    pooled = []
    for g, w in enumerate(POOL_WINDOWS):
        cols = slice(g * pool_w_lanes, (g + 1) * pool_w_lanes)
        cur = vbuf[halo:halo + rows, cols]
        win = cur
        for s in range(1, w):
            win = win + vbuf[halo - s * batch:halo - s * batch + rows, cols]
        cnt = jnp.minimum(w, pos).astype(F32)
        pooled.append(win / cnt - cur)
    branch_pool = _pool_project(pooled, pool_w_ref, pool_scale_ref, pool_out_ref)

    u_bf = u.astype(BF16)
    for j in range(n_slabs):
        s_buf[:, 2 * SLAB_STATES * j:2 * SLAB_STATES * (j + 1)] = _dot(
            u_bf[:, LANES * j:LANES * (j + 1)], b_cat_ref[j])

    slabs_per_loop = 2
    for j0 in range(0, n_slabs, slabs_per_loop):
        js = list(range(j0, j0 + slabs_per_loop))
        lam = []
        for j in js:
            st = slice(SLAB_STATES * j, SLAB_STATES * (j + 1))
            lam.append((jnp.broadcast_to(lam_re_ref[:, st], (batch, SLAB_STATES)),
                        jnp.broadcast_to(lam_im_ref[:, st], (batch, SLAB_STATES))))

        def re_cols(j):
            return slice(2 * SLAB_STATES * j, 2 * SLAB_STATES * j + SLAB_STATES)

        def im_cols(j):
            return slice(2 * SLAB_STATES * j + SLAB_STATES, 2 * SLAB_STATES * (j + 1))

        def body(t, carry, js=js, lam=lam):
            r0 = pl.multiple_of(t * batch, batch)
            out = []
            for k, j in enumerate(js):
                hr, hi = carry[2 * k], carry[2 * k + 1]
                lr, li = lam[k]
                nr = lr * hr - li * hi + s_buf[pl.ds(r0, batch), re_cols(j)]
                ni = lr * hi + li * hr + s_buf[pl.ds(r0, batch), im_cols(j)]
                s_buf[pl.ds(r0, batch), re_cols(j)] = nr
                s_buf[pl.ds(r0, batch), im_cols(j)] = ni
                out += [nr, ni]
            return tuple(out)

        carry0 = []
        for j in js:
            carry0 += [h_state[:, re_cols(j)], h_state[:, im_cols(j)]]
        fin = lax.fori_loop(0, steps, body, tuple(carry0), unroll=4)
        for k, j in enumerate(js):
            h_state[:, re_cols(j)] = fin[2 * k]
            h_state[:, im_cols(j)] = fin[2 * k + 1]

    ys = jnp.concatenate(
        [_dot(s_buf[:, 2 * SLAB_STATES * j:2 * SLAB_STATES * (j + 1)].astype(BF16), c_cat_ref[j])
         for j in range(n_slabs)], axis=-1)
    ys = ys + d_skip_ref[...] * u

    y_ref[...] = _finish_layer(x, h1, branch_pool, ys, w_in_ref, glu_ref, w_out_ref, g2_ref,
                               w_up_ref, w_down_ref, gf_ref)

    tail = vbuf[rows:rows + halo, :]
    vbuf[0:halo, :] = tail
    pool_tail_ref[...] = tail
    h_fin_ref[...] = h_state[...]


def _const_spec(shape):
    return pl.BlockSpec(shape, lambda i: (0,) * len(shape), pipeline_mode=pl.Buffered(1))


def _prompt_call(x_tm, batch, weights):
    rows_total, d = x_tm.shape
    rows = STEPS_PER_TILE * batch
    n_tiles = rows_total // rows
    pool_width = weights[3].shape[-1]
    n_states2 = 2 * weights[5].shape[-1]
    halo = HALO_STEPS * batch
    out_shape = (jax.ShapeDtypeStruct((rows_total, d), F32),
                 jax.ShapeDtypeStruct((halo, pool_width), F32),
                 jax.ShapeDtypeStruct((batch, n_states2), F32))
    return pl.pallas_call(
        _prompt_kernel,
        out_shape=out_shape,
        grid=(n_tiles,),
        in_specs=[pl.BlockSpec((rows, d), lambda i: (i, 0))] + [_const_spec(w.shape) for w in weights],
        out_specs=[pl.BlockSpec((rows, d), lambda i: (i, 0)),
                   pl.BlockSpec((halo, pool_width), lambda i: (0, 0)),
                   pl.BlockSpec((batch, n_states2), lambda i: (0, 0))],
        scratch_shapes=[pltpu.VMEM((halo + rows, pool_width), F32),
                        pltpu.VMEM((rows, n_states2), F32),
                        pltpu.VMEM((batch, n_states2), F32)],
        compiler_params=pltpu.CompilerParams(dimension_semantics=("arbitrary",),
                                             vmem_limit_bytes=VMEM_LIMIT_BYTES),
        name="prompt_layer",
    )(x_tm, *weights)


def _sample_kernel(x_ref, cache_ref, h0_re_ref, h0_im_ref,
                   g1_ref, w_in_ref, pool_w_ref, pool_scale_ref, pool_out_ref,
                   lam_re_ref, lam_im_ref, b_cat_ref, c_cat_ref, d_skip_ref, glu_ref,
                   w_out_ref, g2_ref, w_up_ref, w_down_ref, gf_ref,
                   y_ref, v_ref, h_re_ref, h_im_ref):
    n_slabs = b_cat_ref.shape[0]
    x = x_ref[...]
    rows = x.shape[0]
    h1 = _rmsnorm(x, g1_ref[...]).astype(BF16)
    pool_width = v_ref.shape[1]
    v = _dot(h1, w_in_ref[:, 0:pool_width])
    v_ref[...] = v
    u = _dot(h1, w_in_ref[:, pool_width:2 * pool_width])

    pooled = []
    for g, w in enumerate(POOL_WINDOWS):
        cols = slice(g * LANES, (g + 1) * LANES)
        cur = v[:, cols]
        win = cur
        for s in range(1, w):
            win = win + cache_ref[POOL_BUF - s, :, cols]
        cnt = float(min(w, PAST_LEN + 1))
        pooled.append(win / cnt - cur)
    branch_pool = _pool_project(pooled, pool_w_ref, pool_scale_ref, pool_out_ref)

    u_bf = u.astype(BF16)
    ys_parts = []
    for j in range(n_slabs):
        st = slice(SLAB_STATES * j, SLAB_STATES * (j + 1))
        bu = _dot(u_bf[:, LANES * j:LANES * (j + 1)], b_cat_ref[j])
        lr = lam_re_ref[:, st]
        li = lam_im_ref[:, st]
        hr = h0_re_ref[:, st]
        hi = h0_im_ref[:, st]
        nr = lr * hr - li * hi + bu[:, :SLAB_STATES]
        ni = lr * hi + li * hr + bu[:, SLAB_STATES:]
        h_re_ref[:, st] = nr
        h_im_ref[:, st] = ni
        hcat = jnp.concatenate([nr, ni], axis=-1).astype(BF16)
        ys_parts.append(_dot(hcat, c_cat_ref[j]))
    ys = jnp.concatenate(ys_parts, axis=-1) + d_skip_ref[...] * u

    y_ref[...] = _finish_layer(x, h1, branch_pool, ys, w_in_ref, glu_ref, w_out_ref, g2_ref,
                               w_up_ref, w_down_ref, gf_ref)


def _sample_call(x, cache_tm, h0_re, h0_im, weights):
    rows, d = x.shape
    pool_width = cache_tm.shape[-1]
    n_states = h0_re.shape[-1]
    ins = (x, cache_tm, h0_re, h0_im) + tuple(weights)
    out_shape = (jax.ShapeDtypeStruct((rows, d), F32),
                 jax.ShapeDtypeStruct((rows, pool_width), F32),
                 jax.ShapeDtypeStruct((rows, n_states), F32),
                 jax.ShapeDtypeStruct((rows, n_states), F32))
    full = lambda shape: pl.BlockSpec(shape, lambda i: (0,) * len(shape))
    return pl.pallas_call(
        _sample_kernel,
        out_shape=out_shape,
        grid=(1,),
        in_specs=[full(a.shape) for a in ins],
        out_specs=[full(o.shape) for o in out_shape],
        compiler_params=pltpu.CompilerParams(dimension_semantics=("arbitrary",),
                                             vmem_limit_bytes=VMEM_LIMIT_BYTES),
        name="sample_layer",
    )(*ins)


def kernel(x_prompt, x_sample, cache_pool, state_ssm_re, state_ssm_im, norm1_g, w_in, pool_w,
           pool_scale, pool_out, ssm_a_re, ssm_a_im, ssm_log_dt, ssm_b_re, ssm_b_im, ssm_c_re,
           ssm_c_im, ssm_d, ssm_glu, w_out, norm2_g, w_up, w_down, normf_g):
    depth = w_in.shape[0]
    assert depth == 1, "the final norm is fused into the (single) layer kernel"
    batch, seq, d = x_prompt.shape
    dec_batch, dec_seq, _ = x_sample.shape
    assert dec_seq == 1 and batch == SUBLANES and seq % STEPS_PER_TILE == 0
    groups, n_state = ssm_a_re.shape[1:]
    n_states = groups * n_state
    l = 0

    lam_re, lam_im, bbar_re, bbar_im = _ssm_prep(ssm_a_re[l], ssm_a_im[l], ssm_log_dt[l],
                                                 ssm_b_re[l], ssm_b_im[l])
    b_cat, c_cat = _block_diag_ssm(bbar_re, bbar_im, ssm_c_re[l], ssm_c_im[l])
    row = lambda a: a.reshape(1, -1).astype(F32)
    weights = (row(norm1_g[l]), w_in[l].astype(BF16), pool_w[l].astype(BF16), row(pool_scale[l]),
               pool_out[l].astype(BF16), row(lam_re), row(lam_im), b_cat, c_cat, row(ssm_d[l]),
               ssm_glu[l].astype(BF16), w_out[l].astype(BF16), row(norm2_g[l]),
               w_up[l].astype(BF16), w_down[l].astype(BF16), row(normf_g))

    x_tm = x_prompt.transpose(1, 0, 2).reshape(seq * batch, d)
    y_tm, pool_tail, h_fin = _prompt_call(x_tm, batch, weights)
    y_prompt = y_tm.reshape(seq, batch, d).transpose(1, 0, 2)
    pool_width = pool_tail.shape[-1]
    pool_p = pool_tail.reshape(HALO_STEPS, batch, pool_width)[HALO_STEPS - POOL_BUF:].transpose(1, 0, 2)
    h_fin = h_fin.reshape(batch, n_states // SLAB_STATES, 2, SLAB_STATES)
    re_p = h_fin[:, :, 0].reshape(batch, groups, n_state)
    im_p = h_fin[:, :, 1].reshape(batch, groups, n_state)

    cache_tm = cache_pool[l].transpose(1, 0, 2)
    y_s, v_s, re_s, im_s = _sample_call(
        x_sample.reshape(dec_batch, d), cache_tm,
        state_ssm_re[l].reshape(dec_batch, n_states), state_ssm_im[l].reshape(dec_batch, n_states),
        weights)
    pool_s = jnp.concatenate([cache_pool[l][:, 1:], v_s[:, None, :]], axis=1)

    return (y_prompt, y_s.reshape(dec_batch, dec_seq, d),
            pool_p[None], re_p[None], im_p[None],
            pool_s[None], re_s.reshape(dec_batch, groups, n_state)[None],
            im_s.reshape(dec_batch, groups, n_state)[None])
```

```python
import functools
import math

import jax
import jax.numpy as jnp
from jax import lax
from jax.experimental import pallas as pl
from jax.experimental.pallas import tpu as pltpu

POOL_WINDOWS = (2, 4, 8, 16)
POOL_BUF = max(POOL_WINDOWS) - 1
SSM_GROUP_W = 16
SSM_STATE = 64
RMS_EPS = 1e-6
PAST_LEN = 16384

LANES = 128
SUBLANES = 8
GROUPS_PER_SLAB = LANES // SSM_GROUP_W
SLAB_STATES = GROUPS_PER_SLAB * SSM_STATE
STEPS_PER_TILE = 32
HALO_STEPS = 16
FF_CHUNK = 1024
VMEM_LIMIT_BYTES = 56 * 1024 * 1024

F32 = jnp.float32
BF16 = jnp.bfloat16


def _dot(a, b):
    return jnp.dot(a, b, preferred_element_type=F32)


def _sigmoid(x):
    return 1.0 / (1.0 + jnp.exp(-x))


def _rmsnorm(x, g):
    y = x * lax.rsqrt(jnp.mean(x * x, axis=-1, keepdims=True) + RMS_EPS)
    return y * g


def _discretize(a_re, a_im, dt):
    mag = jnp.exp(a_re * dt)
    ang = a_im * dt
    lam_re = mag * jnp.cos(ang)
    lam_im = mag * jnp.sin(ang)
    p = lam_re - 1.0
    q = lam_im
    den = a_re * a_re + a_im * a_im
    coef_re = (p * a_re + q * a_im) / den
    coef_im = (q * a_re - p * a_im) / den
    return lam_re, lam_im, coef_re, coef_im


def _ssm_prep_kernel(a_re_ref, a_im_ref, a_re_rep_ref, a_im_rep_ref, log_dt_ref,
                     b_re_ref, b_im_ref,
                     lam_re_ref, lam_im_ref, bbar_re_ref, bbar_im_ref):
    dt = jnp.exp(log_dt_ref[...])
    lam_re, lam_im, _, _ = _discretize(a_re_ref[...], a_im_ref[...], dt)
    lam_re_ref[...] = lam_re
    lam_im_ref[...] = lam_im
    _, _, coef_re, coef_im = _discretize(a_re_rep_ref[...], a_im_rep_ref[...], dt)
    b_re = b_re_ref[...]
    b_im = b_im_ref[...]
    bbar_re_ref[...] = coef_re * b_re - coef_im * b_im
    bbar_im_ref[...] = coef_re * b_im + coef_im * b_re


def _ssm_prep(a_re, a_im, log_dt, b_re, b_im):
    g, n = a_re.shape
    c = b_re.shape[-1]
    full = lambda shape: pl.BlockSpec(shape, lambda i: (0,) * len(shape))
    ins = (a_re, a_im, jnp.repeat(a_re, c, axis=1), jnp.repeat(a_im, c, axis=1),
           log_dt.reshape(g, 1), b_re.reshape(g, n * c), b_im.reshape(g, n * c))
    outs = (jax.ShapeDtypeStruct((g, n), F32), jax.ShapeDtypeStruct((g, n), F32),
            jax.ShapeDtypeStruct((g, n * c), F32), jax.ShapeDtypeStruct((g, n * c), F32))
    lam_re, lam_im, bbar_re, bbar_im = pl.pallas_call(
        _ssm_prep_kernel,
        out_shape=outs,
        grid=(1,),
        in_specs=[full(x.shape) for x in ins],
        out_specs=[full(o.shape) for o in outs],
        name="ssm_prep",
    )(*ins)
    return lam_re, lam_im, bbar_re.reshape(g, n, c), bbar_im.reshape(g, n, c)


def _block_diag_ssm(bbar_re, bbar_im, c_re, c_im):
    g, n, c = bbar_re.shape
    slabs = g // GROUPS_PER_SLAB
    eye = jnp.eye(GROUPS_PER_SLAB, dtype=F32)

    def b_blocks(b):
        b = b.reshape(slabs, GROUPS_PER_SLAB, n, c)
        return jnp.einsum('sgnc,gh->sgchn', b, eye).reshape(slabs, LANES, SLAB_STATES)

    def c_blocks(m):
        m = m.reshape(slabs, GROUPS_PER_SLAB, c, n)
        return jnp.einsum('sgcn,gh->sgnhc', m, eye).reshape(slabs, SLAB_STATES, LANES)

    b_cat = jnp.concatenate([b_blocks(bbar_re), b_blocks(bbar_im)], axis=2)
    c_cat = jnp.concatenate([c_blocks(c_re), -c_blocks(c_im)], axis=1)
    return b_cat.astype(BF16), c_cat.astype(BF16)


def _pool_project(pooled, pool_w_ref, pool_scale_ref, pool_out_ref):
    mixed = [_dot(p.astype(BF16), pool_w_ref[g]) for g, p in enumerate(pooled)]
    yp = jnp.concatenate(mixed, axis=-1) * pool_scale_ref[...]
    return _dot(yp.astype(BF16), pool_out_ref[...])


def _finish_layer(x, h1, branch_pool, ys, w_in_ref, glu_ref, w_out_ref, g2_ref,
                  w_up_ref, w_down_ref, gf_ref):
    d = x.shape[-1]
    z =_dot(ys.astype(BF16), glu_ref[...])
    branch_ssm = z[:, :d] * _sigmoid(z[:, d:])
    gate_off = w_in_ref.shape[1] - 2 * d
    g_pool = _sigmoid(_dot(h1, w_in_ref[:, gate_off:gate_off + d]))
    g_ssm = _sigmoid(_dot(h1, w_in_ref[:, gate_off + d:]))
    merged = g_pool * branch_pool + g_ssm * branch_ssm
    x1 = x + _dot(merged.astype(BF16), w_out_ref[...])
    h2 = _rmsnorm(x1, g2_ref[...]).astype(BF16)
    d_ff = w_up_ref.shape[1]
    acc = None
    for c0 in range(0, d_ff, FF_CHUNK):
        a = jnp.maximum(_dot(h2, w_up_ref[:, c0:c0 + FF_CHUNK]), 0.0)
        part = _dot((a * a).astype(BF16), w_down_ref[c0:c0 + FF_CHUNK, :])
        acc = part if acc is None else acc + part
    x2 = x1 + acc
    return _rmsnorm(x2, gf_ref[...])


def _prompt_kernel(n_tiles, x_hbm, g1_ref, w_in_ref, pool_w_ref, pool_scale_ref, pool_out_ref,
                   lam_re_ref, lam_im_ref, b_cat_ref, c_cat_ref, d_skip_ref, glu_ref,
                   w_out_ref, g2_ref, w_up_ref, w_down_ref, gf_ref,
                   y_hbm, pool_tail_ref, h_fin_ref,
                   xbuf, ybuf, sem_in, sem_out, vbuf, s_buf, h_state):
    i = pl.program_id(0)
    _, steps, batch, d_model = xbuf.shape
    rows = steps * batch
    halo = HALO_STEPS * batch
    pool_w_lanes = LANES
    n_slabs = b_cat_ref.shape[0]
    slot = lax.rem(i, 2)

    def in_copy(tile, sl, b):
        return pltpu.make_async_copy(x_hbm.at[b, pl.ds(tile * steps, steps), :],
                                     xbuf.at[sl, :, b, :], sem_in.at[sl, b])

    def out_copy(tile, sl, b):
        return pltpu.make_async_copy(ybuf.at[sl, :, b, :],
                                     y_hbm.at[b, pl.ds(tile * steps, steps), :], sem_out.at[sl, b])

    @pl.when(i == 0)
    def _():
        for b in range(batch):
            in_copy(0, 0, b).start()
        vbuf[0:halo, :] = jnp.zeros((halo, vbuf.shape[1]), F32)
        h_state[...] = jnp.zeros(h_state.shape, F32)

    @pl.when(i + 1 < n_tiles)
    def _():
        for b in range(batch):
            in_copy(i + 1, 1 - slot, b).start()

    for b in range(batch):
        in_copy(i, slot, b).wait()

    x = xbuf[slot].reshape(rows, d_model)
    h1 =_rmsnorm(x, g1_ref[...]).astype(BF16)
    pool_width = vbuf.shape[1]
    v = _dot(h1, w_in_ref[:, 0:pool_width])
    vbuf[halo:halo + rows, :] = v
    u = _dot(h1, w_in_ref[:, pool_width:2 * pool_width])

    row = lax.broadcasted_iota(jnp.int32, (rows, pool_w_lanes), 0)
    pos = i * steps + lax.shift_right_logical(row, int(math.log2(batch))) + 1
    pooled = []
    for g, w in enumerate(POOL_WINDOWS):
        cols = slice(g * pool_w_lanes, (g + 1) * pool_w_lanes)
        cur = vbuf[halo:halo + rows, cols]
        win = cur
        for s in range(1, w):
            win = win + vbuf[halo - s * batch:halo - s * batch + rows, cols]
        cnt = jnp.minimum(w, pos).astype(F32)
        pooled.append(win / cnt - cur)
    branch_pool = _pool_project(pooled, pool_w_ref, pool_scale_ref, pool_out_ref)

    u_bf = u.astype(BF16)
    for j in range(n_slabs):
        s_buf[:, 2 * SLAB_STATES * j:2 * SLAB_STATES * (j + 1)] = _dot(
            u_bf[:, LANES * j:LANES * (j + 1)], b_cat_ref[j])

    slabs_per_loop = 2
    for j0 in range(0, n_slabs, slabs_per_loop):
        js = list(range(j0, j0 + slabs_per_loop))
        lam = []
        for j in js:
            st = slice(SLAB_STATES * j, SLAB_STATES * (j + 1))
            lam.append((jnp.broadcast_to(lam_re_ref[:, st], (batch, SLAB_STATES)),
                        jnp.broadcast_to(lam_im_ref[:, st], (batch, SLAB_STATES))))

        def re_cols(j):
            return slice(2 * SLAB_STATES * j, 2 * SLAB_STATES * j + SLAB_STATES)

        def im_cols(j):
            return slice(2 * SLAB_STATES * j + SLAB_STATES, 2 * SLAB_STATES * (j + 1))

        def body(t, carry, js=js, lam=lam):
            r0 = pl.multiple_of(t * batch, batch)
            out = []
            for k, j in enumerate(js):
                hr, hi = carry[2 * k], carry[2 * k + 1]
                lr, li = lam[k]
                nr = lr * hr - li * hi + s_buf[pl.ds(r0, batch), re_cols(j)]
                ni = lr * hi + li * hr + s_buf[pl.ds(r0, batch), im_cols(j)]
                s_buf[pl.ds(r0, batch), re_cols(j)] = nr
                s_buf[pl.ds(r0, batch), im_cols(j)] = ni
                out += [nr, ni]
            return tuple(out)

        carry0 = []
        for j in js:
            carry0 += [h_state[:, re_cols(j)], h_state[:, im_cols(j)]]
        fin = lax.fori_loop(0, steps, body, tuple(carry0), unroll=4)
        for k, j in enumerate(js):
            h_state[:, re_cols(j)] = fin[2 * k]
            h_state[:, im_cols(j)] = fin[2 * k + 1]

    ys = jnp.concatenate(
        [_dot(s_buf[:, 2 * SLAB_STATES * j:2 * SLAB_STATES * (j + 1)].astype(BF16), c_cat_ref[j])
         for j in range(n_slabs)], axis=-1)
    ys = ys + d_skip_ref[...] * u

    y = _finish_layer(x, h1, branch_pool, ys, w_in_ref, glu_ref, w_out_ref, g2_ref,
                      w_up_ref, w_down_ref, gf_ref)

    @pl.when(i >= 2)
    def _():
        for b in range(batch):
            out_copy(i - 2, slot, b).wait()

    ybuf[slot] = y.reshape(steps, batch, d_model)
    for b in range(batch):
        out_copy(i, slot, b).start()

    @pl.when(i == n_tiles - 1)
    def _():
        if n_tiles >= 2:
            for b in range(batch):
                out_copy(i - 1, 1 - slot, b).wait()
        for b in range(batch):
            out_copy(i, slot, b).wait()

    tail = vbuf[rows:rows + halo, :]
    vbuf[0:halo, :] = tail
    pool_tail_ref[...] = tail
    h_fin_ref[...] = h_state[...]


def _const_spec(shape):
    return pl.BlockSpec(shape, lambda i: (0,) * len(shape), pipeline_mode=pl.Buffered(1))


def _prompt_call(x, weights):
    batch, seq, d = x.shape
    steps = STEPS_PER_TILE
    rows = steps * batch
    n_tiles = seq // steps
    pool_width = weights[3].shape[-1]
    n_states2 = 2 * weights[5].shape[-1]
    halo = HALO_STEPS * batch
    out_shape = (jax.ShapeDtypeStruct((batch, seq, d), F32),
                 jax.ShapeDtypeStruct((halo, pool_width), F32),
                 jax.ShapeDtypeStruct((batch, n_states2), F32))
    return pl.pallas_call(
        functools.partial(_prompt_kernel, n_tiles),
        out_shape=out_shape,
        grid=(n_tiles,),
        in_specs=[pl.BlockSpec(memory_space=pl.ANY)] + [_const_spec(w.shape) for w in weights],
        out_specs=[pl.BlockSpec(memory_space=pl.ANY),
                   pl.BlockSpec((halo, pool_width), lambda i: (0, 0)),
                   pl.BlockSpec((batch, n_states2), lambda i: (0, 0))],
        scratch_shapes=[pltpu.VMEM((2, steps, batch, d), F32),
                        pltpu.VMEM((2, steps, batch, d), F32),
                        pltpu.SemaphoreType.DMA((2, batch)),
                        pltpu.SemaphoreType.DMA((2, batch)),
                        pltpu.VMEM((halo + rows, pool_width), F32),
                        pltpu.VMEM((rows, n_states2), F32),
                        pltpu.VMEM((batch, n_states2), F32)],
        compiler_params=pltpu.CompilerParams(dimension_semantics=("arbitrary",),
                                             vmem_limit_bytes=VMEM_LIMIT_BYTES),
        name="prompt_layer",
    )(x, *weights)


def _sample_kernel(x_ref, cache_ref, h0_re_ref, h0_im_ref,
                   g1_ref, w_in_ref, pool_w_ref, pool_scale_ref, pool_out_ref,
                   lam_re_ref, lam_im_ref, b_cat_ref, c_cat_ref, d_skip_ref, glu_ref,
                   w_out_ref, g2_ref, w_up_ref, w_down_ref, gf_ref,
                   y_ref, v_ref, h_re_ref, h_im_ref):
    n_slabs = b_cat_ref.shape[0]
    x = x_ref[...]
    rows = x.shape[0]
    h1 = _rmsnorm(x, g1_ref[...]).astype(BF16)
    pool_width = v_ref.shape[1]
    v = _dot(h1, w_in_ref[:, 0:pool_width])
    v_ref[...] = v
    u = _dot(h1, w_in_ref[:, pool_width:2 * pool_width])

    pooled = []
    for g, w in enumerate(POOL_WINDOWS):
        cols = slice(g * LANES, (g + 1) * LANES)
        cur = v[:, cols]
        win = cur
        for s in range(1, w):
            win = win + cache_ref[POOL_BUF - s, :, cols]
        cnt = float(min(w, PAST_LEN + 1))
        pooled.append(win / cnt - cur)
    branch_pool = _pool_project(pooled, pool_w_ref, pool_scale_ref, pool_out_ref)

    u_bf = u.astype(BF16)
    ys_parts = []
    for j in range(n_slabs):
        st = slice(SLAB_STATES * j, SLAB_STATES * (j + 1))
        bu = _dot(u_bf[:, LANES * j:LANES * (j + 1)], b_cat_ref[j])
        lr = lam_re_ref[:, st]
        li = lam_im_ref[:, st]
        hr = h0_re_ref[:, st]
        hi = h0_im_ref[:, st]
        nr = lr * hr - li * hi + bu[:, :SLAB_STATES]
        ni = lr * hi + li * hr + bu[:, SLAB_STATES:]
        h_re_ref[:, st] = nr
        h_im_ref[:, st] = ni
        hcat = jnp.concatenate([nr, ni], axis=-1).astype(BF16)
        ys_parts.append(_dot(hcat, c_cat_ref[j]))
    ys = jnp.concatenate(ys_parts, axis=-1) + d_skip_ref[...] * u

    y_ref[...] = _finish_layer(x, h1, branch_pool, ys, w_in_ref, glu_ref, w_out_ref, g2_ref,
                               w_up_ref, w_down_ref, gf_ref)


def _sample_call(x, cache_tm, h0_re, h0_im, weights):
    rows, d = x.shape
    pool_width = cache_tm.shape[-1]
    n_states = h0_re.shape[-1]
    ins = (x, cache_tm, h0_re, h0_im) + tuple(weights)
    out_shape = (jax.ShapeDtypeStruct((rows, d), F32),
                 jax.ShapeDtypeStruct((rows, pool_width), F32),
                 jax.ShapeDtypeStruct((rows, n_states), F32),
                 jax.ShapeDtypeStruct((rows, n_states), F32))
    full = lambda shape: pl.BlockSpec(shape, lambda i: (0,) * len(shape))
    return pl.pallas_call(
        _sample_kernel,
        out_shape=out_shape,
        grid=(1,),
        in_specs=[full(a.shape) for a in ins],
        out_specs=[full(o.shape) for o in out_shape],
        compiler_params=pltpu.CompilerParams(dimension_semantics=("arbitrary",),
                                             vmem_limit_bytes=VMEM_LIMIT_BYTES),
        name="sample_layer",
    )(*ins)


def kernel(x_prompt, x_sample, cache_pool, state_ssm_re, state_ssm_im, norm1_g, w_in, pool_w,
           pool_scale, pool_out, ssm_a_re, ssm_a_im, ssm_log_dt, ssm_b_re, ssm_b_im, ssm_c_re,
           ssm_c_im, ssm_d, ssm_glu, w_out, norm2_g, w_up, w_down, normf_g):
    depth = w_in.shape[0]
    assert depth == 1, "the final norm is fused into the (single) layer kernel"
    batch, seq, d = x_prompt.shape
    dec_batch, dec_seq, _ = x_sample.shape
    assert dec_seq == 1 and batch == SUBLANES and seq % STEPS_PER_TILE == 0
    groups, n_state = ssm_a_re.shape[1:]
    n_states = groups * n_state
    l = 0

    lam_re, lam_im, bbar_re, bbar_im = _ssm_prep(ssm_a_re[l], ssm_a_im[l], ssm_log_dt[l],
                                                 ssm_b_re[l], ssm_b_im[l])
    b_cat, c_cat = _block_diag_ssm(bbar_re, bbar_im, ssm_c_re[l], ssm_c_im[l])
    row = lambda a: a.reshape(1, -1).astype(F32)
    weights = (row(norm1_g[l]), w_in[l].astype(BF16), pool_w[l].astype(BF16), row(pool_scale[l]),
               pool_out[l].astype(BF16), row(lam_re), row(lam_im), b_cat, c_cat, row(ssm_d[l]),
               ssm_glu[l].astype(BF16), w_out[l].astype(BF16), row(norm2_g[l]),
               w_up[l].astype(BF16), w_down[l].astype(BF16), row(normf_g))

    y_prompt, pool_tail, h_fin = _prompt_call(x_prompt, weights)
    pool_width = pool_tail.shape[-1]
    pool_p = pool_tail.reshape(HALO_STEPS, batch, pool_width)[HALO_STEPS - POOL_BUF:].transpose(1, 0, 2)
    h_fin = h_fin.reshape(batch, n_states // SLAB_STATES, 2, SLAB_STATES)
    re_p = h_fin[:, :, 0].reshape(batch, groups, n_state)
    im_p = h_fin[:, :, 1].reshape(batch, groups, n_state)

    cache_tm = cache_pool[l].transpose(1, 0, 2)
    y_s, v_s, re_s, im_s = _sample_call(
        x_sample.reshape(dec_batch, d), cache_tm,
        state_ssm_re[l].reshape(dec_batch, n_states), state_ssm_im[l].reshape(dec_batch, n_states),
        weights)
    pool_s = jnp.concatenate([cache_pool[l][:, 1:], v_s[:, None, :]], axis=1)

    return (y_prompt, y_s.reshape(dec_batch, dec_seq, d),
            pool_p[None], re_p[None], im_p[None],
            pool_s[None], re_s.reshape(dec_batch, groups, n_state)[None],
            im_s.reshape(dec_batch, groups, n_state)[None])
```

```python
import functools
import math

import jax
import jax.numpy as jnp
from jax import lax
from jax.experimental import pallas as pl
from jax.experimental.pallas import tpu as pltpu

POOL_WINDOWS = (2, 4, 8, 16)
POOL_BUF = max(POOL_WINDOWS) - 1
SSM_GROUP_W = 16
SSM_STATE = 64
RMS_EPS = 1e-6
PAST_LEN = 16384

LANES = 128
SUBLANES = 8
GROUPS_PER_SLAB = LANES // SSM_GROUP_W
SLAB_STATES = GROUPS_PER_SLAB * SSM_STATE
STEPS_PER_TILE = 32
HALO_STEPS = 16
X_SLOTS = 3
FF_CHUNK = 1024
VMEM_LIMIT_BYTES = 56 * 1024 * 1024

F32 = jnp.float32
BF16 = jnp.bfloat16


def _dot(a, b):
    return jnp.dot(a, b, preferred_element_type=F32)


def _sigmoid(x):
    return 1.0 / (1.0 + jnp.exp(-x))


def _rmsnorm(x, g):
    y = x * lax.rsqrt(jnp.mean(x * x, axis=-1, keepdims=True) + RMS_EPS)
    return y * g


def _discretize(a_re, a_im, dt):
    mag = jnp.exp(a_re * dt)
    ang = a_im * dt
    lam_re = mag * jnp.cos(ang)
    lam_im = mag * jnp.sin(ang)
    p = lam_re - 1.0
    q = lam_im
    den = a_re * a_re + a_im * a_im
    coef_re = (p * a_re + q * a_im) / den
    coef_im = (q * a_re - p * a_im) / den
    return lam_re, lam_im, coef_re, coef_im


def _ssm_prep_kernel(a_re_ref, a_im_ref, a_re_rep_ref, a_im_rep_ref, log_dt_ref,
                     b_re_ref, b_im_ref,
                     lam_re_ref, lam_im_ref, bbar_re_ref, bbar_im_ref):
    dt = jnp.exp(log_dt_ref[...])
    lam_re, lam_im, _, _ = _discretize(a_re_ref[...], a_im_ref[...], dt)
    lam_re_ref[...] = lam_re
    lam_im_ref[...] = lam_im
    _, _, coef_re, coef_im = _discretize(a_re_rep_ref[...], a_im_rep_ref[...], dt)
    b_re = b_re_ref[...]
    b_im = b_im_ref[...]
    bbar_re_ref[...] = coef_re * b_re - coef_im * b_im
    bbar_im_ref[...] = coef_re * b_im + coef_im * b_re


def _ssm_prep(a_re, a_im, log_dt, b_re, b_im):
    g, n = a_re.shape
    c = b_re.shape[-1]
    full = lambda shape: pl.BlockSpec(shape, lambda i: (0,) * len(shape))
    ins = (a_re, a_im, jnp.repeat(a_re, c, axis=1), jnp.repeat(a_im, c, axis=1),
           log_dt.reshape(g, 1), b_re.reshape(g, n * c), b_im.reshape(g, n * c))
    outs = (jax.ShapeDtypeStruct((g, n), F32), jax.ShapeDtypeStruct((g, n), F32),
            jax.ShapeDtypeStruct((g, n * c), F32), jax.ShapeDtypeStruct((g, n * c), F32))
    lam_re, lam_im, bbar_re, bbar_im = pl.pallas_call(
        _ssm_prep_kernel,
        out_shape=outs,
        grid=(1,),
        in_specs=[full(x.shape) for x in ins],
        out_specs=[full(o.shape) for o in outs],
        name="ssm_prep",
    )(*ins)
    return lam_re, lam_im, bbar_re.reshape(g, n, c), bbar_im.reshape(g, n, c)


def _block_diag_ssm(bbar_re, bbar_im, c_re, c_im):
    g, n, c = bbar_re.shape
    slabs = g // GROUPS_PER_SLAB
    eye = jnp.eye(GROUPS_PER_SLAB, dtype=F32)

    def b_blocks(b):
        b = b.reshape(slabs, GROUPS_PER_SLAB, n, c)
        return jnp.einsum('sgnc,gh->sgchn', b, eye).reshape(slabs, LANES, SLAB_STATES)

    def c_blocks(m):
        m = m.reshape(slabs, GROUPS_PER_SLAB, c, n)
        return jnp.einsum('sgcn,gh->sgnhc', m, eye).reshape(slabs, SLAB_STATES, LANES)

    b_cat = jnp.concatenate([b_blocks(bbar_re), b_blocks(bbar_im)], axis=2)
    c_cat = jnp.concatenate([c_blocks(c_re), -c_blocks(c_im)], axis=1)
    return b_cat.astype(BF16), c_cat.astype(BF16)


def _pool_project(pooled, pool_w_ref, pool_scale_ref, pool_out_ref):
    mixed = [_dot(p.astype(BF16), pool_w_ref[g]) for g, p in enumerate(pooled)]
    yp = jnp.concatenate(mixed, axis=-1) * pool_scale_ref[...]
    return _dot(yp.astype(BF16), pool_out_ref[...])


def _finish_layer(x, h1, branch_pool, ys, w_in_ref, glu_ref, w_out_ref, g2_ref,
                  w_up_ref, w_down_ref):
    d = x.shape[-1]
    z = _dot(ys.astype(BF16), glu_ref[...])
    branch_ssm = z[:, :d] * _sigmoid(z[:, d:])
    gate_off = w_in_ref.shape[1] - 2 * d
    g_pool = _sigmoid(_dot(h1, w_in_ref[:, gate_off:gate_off + d]))
    g_ssm = _sigmoid(_dot(h1, w_in_ref[:, gate_off + d:]))
    merged = g_pool * branch_pool + g_ssm * branch_ssm
    x1 = x + _dot(merged.astype(BF16), w_out_ref[...])
    h2 = _rmsnorm(x1, g2_ref[...]).astype(BF16)
    d_ff = w_up_ref.shape[1]
    acc = None
    for c0 in range(0, d_ff, FF_CHUNK):
        a = jnp.maximum(_dot(h2, w_up_ref[:, c0:c0 + FF_CHUNK]), 0.0)
        part = _dot((a * a).astype(BF16), w_down_ref[c0:c0 + FF_CHUNK, :])
        acc = part if acc is None else acc + part
    return x1 + acc


def _prompt_kernel(n_tiles, x_hbm, g1_ref, w_in_ref, pool_w_ref, pool_scale_ref, pool_out_ref,
                   lam_re_ref, lam_im_ref, b_cat_ref, c_cat_ref, d_skip_ref, glu_ref,
                   w_out_ref, g2_ref, w_up_ref, w_down_ref, gf_ref,
                   y_hbm, pool_tail_ref, h_fin_ref,
                   xbuf, ybuf, sem_in, sem_out, h1buf, x2buf, vbuf, s_buf, h_state):
    i = pl.program_id(0)
    n_x, steps, batch, d_model = xbuf.shape
    rows = steps * batch
    halo = HALO_STEPS * batch
    n_slabs = b_cat_ref.shape[0]
    xslot = lax.rem(i, n_x)
    xslot_next = lax.rem(i + 1, n_x)
    par = lax.rem(i, 2)

    def in_copy(tile, b):
        sl = lax.rem(tile, n_x)
        return pltpu.make_async_copy(x_hbm.at[b, pl.ds(tile * steps, steps), :],
                                     xbuf.at[sl, :, b, :], sem_in.at[sl, b])

    def out_copy(tile, b):
        sl = lax.rem(tile, 2)
        return pltpu.make_async_copy(ybuf.at[sl, :, b, :],
                                     y_hbm.at[b, pl.ds(tile * steps, steps), :], sem_out.at[sl, b])

    def final_norm_to_ybuf(sl):
        y = _rmsnorm(x2buf[sl], gf_ref[...])
        ybuf[sl] = y.reshape(steps, batch, d_model)

    @pl.when(i == 0)
    def _():
        for b in range(batch):
            in_copy(0, b).start()
        for b in range(batch):
            in_copy(1, b).start()
        vbuf[0:halo, :] = jnp.zeros((halo, vbuf.shape[1]), F32)
        h_state[...] = jnp.zeros(h_state.shape, F32)
        x2buf[1] = jnp.zeros(x2buf.shape[1:], F32)
        for b in range(batch):
            in_copy(0, b).wait()
        h1buf[0] = _rmsnorm(xbuf[0].reshape(rows, d_model), g1_ref[...]).astype(BF16)

    @pl.when(i + 2 < n_tiles)
    def _():
        for b in range(batch):
            in_copy(i + 2, b).start()

    @pl.when(i + 1 < n_tiles)
    def _():
        for b in range(batch):
            in_copy(i + 1, b).wait()

    @pl.when(i >= 3)
    def _():
        for b in range(batch):
            out_copy(i - 3, b).wait()

    x = xbuf[xslot].reshape(rows, d_model)
    h1 = h1buf[par]
    pool_width = vbuf.shape[1]
    v = _dot(h1, w_in_ref[:, 0:pool_width])
    vbuf[halo:halo + rows, :] = v
    u = _dot(h1, w_in_ref[:, pool_width:2 * pool_width])

    row = lax.broadcasted_iota(jnp.int32, (rows, LANES), 0)
    pos = i * steps + lax.shift_right_logical(row, int(math.log2(batch))) + 1
    pooled = []
    for g, w in enumerate(POOL_WINDOWS):
        cols = slice(g * LANES, (g + 1) * LANES)
        cur = vbuf[halo:halo + rows, cols]
        win = cur
        for s in range(1, w):
            win = win + vbuf[halo - s * batch:halo - s * batch + rows, cols]
        cnt = jnp.minimum(w, pos).astype(F32)
        pooled.append(win / cnt - cur)
    branch_pool = _pool_project(pooled, pool_w_ref, pool_scale_ref, pool_out_ref)

    u_bf = u.astype(BF16)
    for j in range(n_slabs):
        s_buf[:, 2 * SLAB_STATES * j:2 * SLAB_STATES * (j + 1)] = _dot(
            u_bf[:, LANES * j:LANES * (j + 1)], b_cat_ref[j])
    for j in range(n_slabs):
        st = slice(SLAB_STATES * j, SLAB_STATES * (j + 1))
        re_cols = slice(2 * SLAB_STATES * j, 2 * SLAB_STATES * j + SLAB_STATES)
        im_cols = slice(2 * SLAB_STATES * j + SLAB_STATES, 2 * SLAB_STATES * (j + 1))
        lr = jnp.broadcast_to(lam_re_ref[:, st], (batch, SLAB_STATES))
        li = jnp.broadcast_to(lam_im_ref[:, st], (batch, SLAB_STATES))
        hr = h_state[:, re_cols]
        hi = h_state[:, im_cols]
        for t in range(steps):
            r = slice(t * batch, (t + 1) * batch)
            hr, hi = (lr * hr - li * hi + s_buf[r, re_cols],
                      lr * hi + li * hr + s_buf[r, im_cols])
            s_buf[r, re_cols] = hr
            s_buf[r, im_cols] = hi
        h_state[:, re_cols] = hr
        h_state[:, im_cols] = hi
    ys = jnp.concatenate(
        [_dot(s_buf[:, 2 * SLAB_STATES * j:2 * SLAB_STATES * (j + 1)].astype(BF16), c_cat_ref[j])
         for j in range(n_slabs)], axis=-1)
    ys = ys + d_skip_ref[...] * u

    x2buf[par] = _finish_layer(x, h1, branch_pool, ys, w_in_ref, glu_ref, w_out_ref, g2_ref,
                               w_up_ref, w_down_ref)

    tail = vbuf[rows:rows + halo, :]
    vbuf[0:halo, :] = tail
    pool_tail_ref[...] = tail
    h_fin_ref[...] = h_state[...]

    h1buf[1 - par] = _rmsnorm(xbuf[xslot_next].reshape(rows, d_model), g1_ref[...]).astype(BF16)

    final_norm_to_ybuf(1 - par)

    @pl.when(i >= 1)
    def _():
        for b in range(batch):
            out_copy(i - 1, b).start()

    @pl.when(i == n_tiles - 1)
    def _():
        for b in range(batch):
            out_copy(i - 2, b).wait()
        final_norm_to_ybuf(par)
        for b in range(batch):
            out_copy(i, b).start()
        for b in range(batch):
            out_copy(i - 1, b).wait()
        for b in range(batch):
            out_copy(i, b).wait()


def _const_spec(shape):
    return pl.BlockSpec(shape, lambda i: (0,) * len(shape), pipeline_mode=pl.Buffered(1))


def _prompt_call(x, weights):
    batch, seq, d = x.shape
    steps = STEPS_PER_TILE
    rows = steps * batch
    n_tiles = seq // steps
    assert n_tiles >= X_SLOTS, "the input ring and the deferred final norm assume at least 3 tiles"
    pool_width = weights[3].shape[-1]
    n_states2 = 2 * weights[5].shape[-1]
    halo = HALO_STEPS * batch
    out_shape = (jax.ShapeDtypeStruct((batch, seq, d), F32),
                 jax.ShapeDtypeStruct((halo, pool_width), F32),
                 jax.ShapeDtypeStruct((batch, n_states2), F32))
    return pl.pallas_call(
        functools.partial(_prompt_kernel, n_tiles),
        out_shape=out_shape,
        grid=(n_tiles,),
        in_specs=[pl.BlockSpec(memory_space=pl.ANY)] + [_const_spec(w.shape) for w in weights],
        out_specs=[pl.BlockSpec(memory_space=pl.ANY),
                   pl.BlockSpec((halo, pool_width), lambda i: (0, 0)),
                   pl.BlockSpec((batch, n_states2), lambda i: (0, 0))],
        scratch_shapes=[pltpu.VMEM((X_SLOTS, steps, batch, d), F32),
                        pltpu.VMEM((2, steps, batch, d), F32),
                        pltpu.SemaphoreType.DMA((X_SLOTS, batch)),
                        pltpu.SemaphoreType.DMA((2, batch)),
                        pltpu.VMEM((2, rows, d), BF16),
                        pltpu.VMEM((2, rows, d), F32),
                        pltpu.VMEM((halo + rows, pool_width), F32),
                        pltpu.VMEM((rows, n_states2), F32),
                        pltpu.VMEM((batch, n_states2), F32)],
        compiler_params=pltpu.CompilerParams(dimension_semantics=("arbitrary",),
                                             vmem_limit_bytes=VMEM_LIMIT_BYTES),
        name="prompt_layer",
    )(x, *weights)


def _sample_kernel(x_ref, cache_ref, h0_re_ref, h0_im_ref,
                   g1_ref, w_in_ref, pool_w_ref, pool_scale_ref, pool_out_ref,
                   lam_re_ref, lam_im_ref, b_cat_ref, c_cat_ref, d_skip_ref, glu_ref,
                   w_out_ref, g2_ref, w_up_ref, w_down_ref, gf_ref,
                   y_ref, v_ref, h_re_ref, h_im_ref):
    n_slabs = b_cat_ref.shape[0]
    x = x_ref[...]
    h1 = _rmsnorm(x, g1_ref[...]).astype(BF16)
    pool_width = v_ref.shape[1]
    v = _dot(h1, w_in_ref[:, 0:pool_width])
    v_ref[...] = v
    u = _dot(h1, w_in_ref[:, pool_width:2 * pool_width])

    pooled = []
    for g, w in enumerate(POOL_WINDOWS):
        cols = slice(g * LANES, (g + 1) * LANES)
        cur = v[:, cols]
        win = cur
        for s in range(1, w):
            win = win + cache_ref[POOL_BUF - s, :, cols]
        cnt = float(min(w, PAST_LEN + 1))
        pooled.append(win / cnt - cur)
    branch_pool = _pool_project(pooled, pool_w_ref, pool_scale_ref, pool_out_ref)

    u_bf = u.astype(BF16)
    ys_parts = []
    for j in range(n_slabs):
        st = slice(SLAB_STATES * j, SLAB_STATES * (j + 1))
        bu = _dot(u_bf[:, LANES * j:LANES * (j + 1)], b_cat_ref[j])
        lr = lam_re_ref[:, st]
        li = lam_im_ref[:, st]
        hr = h0_re_ref[:, st]
        hi = h0_im_ref[:, st]
        nr = lr * hr - li * hi + bu[:, :SLAB_STATES]
        ni = lr * hi + li * hr + bu[:, SLAB_STATES:]
        h_re_ref[:, st] = nr
        h_im_ref[:, st] = ni
        hcat = jnp.concatenate([nr, ni], axis=-1).astype(BF16)
        ys_parts.append(_dot(hcat, c_cat_ref[j]))
    ys = jnp.concatenate(ys_parts, axis=-1) + d_skip_ref[...] * u

    x2 = _finish_layer(x, h1, branch_pool, ys, w_in_ref, glu_ref, w_out_ref, g2_ref,
                       w_up_ref, w_down_ref)
    y_ref[...] = _rmsnorm(x2, gf_ref[...])


def _sample_call(x, cache_tm, h0_re, h0_im, weights):
    rows, d = x.shape
    pool_width = cache_tm.shape[-1]
    n_states = h0_re.shape[-1]
    ins = (x, cache_tm, h0_re, h0_im) + tuple(weights)
    out_shape = (jax.ShapeDtypeStruct((rows, d), F32),
                 jax.ShapeDtypeStruct((rows, pool_width), F32),
                 jax.ShapeDtypeStruct((rows, n_states), F32),
                 jax.ShapeDtypeStruct((rows, n_states), F32))
    full = lambda shape: pl.BlockSpec(shape, lambda i: (0,) * len(shape))
    return pl.pallas_call(
        _sample_kernel,
        out_shape=out_shape,
        grid=(1,),
        in_specs=[full(a.shape) for a in ins],
        out_specs=[full(o.shape) for o in out_shape],
        compiler_params=pltpu.CompilerParams(dimension_semantics=("arbitrary",),
                                             vmem_limit_bytes=VMEM_LIMIT_BYTES),
        name="sample_layer",
    )(*ins)


def kernel(x_prompt, x_sample, cache_pool, state_ssm_re, state_ssm_im, norm1_g, w_in, pool_w,
           pool_scale, pool_out, ssm_a_re, ssm_a_im, ssm_log_dt, ssm_b_re, ssm_b_im, ssm_c_re,
           ssm_c_im, ssm_d, ssm_glu, w_out, norm2_g, w_up, w_down, normf_g):
    depth = w_in.shape[0]
    assert depth == 1, "the final norm is fused into the (single) layer kernel"
    batch, seq, d = x_prompt.shape
    dec_batch, dec_seq, _ = x_sample.shape
    assert dec_seq == 1 and batch == SUBLANES and seq % STEPS_PER_TILE == 0
    groups, n_state = ssm_a_re.shape[1:]
    n_states = groups * n_state
    l = 0

    lam_re, lam_im, bbar_re, bbar_im = _ssm_prep(ssm_a_re[l], ssm_a_im[l], ssm_log_dt[l],
                                                 ssm_b_re[l], ssm_b_im[l])
    b_cat, c_cat = _block_diag_ssm(bbar_re, bbar_im, ssm_c_re[l], ssm_c_im[l])
    row = lambda a: a.reshape(1, -1).astype(F32)
    weights = (row(norm1_g[l]), w_in[l].astype(BF16), pool_w[l].astype(BF16), row(pool_scale[l]),
               pool_out[l].astype(BF16), row(lam_re), row(lam_im), b_cat, c_cat, row(ssm_d[l]),
               ssm_glu[l].astype(BF16), w_out[l].astype(BF16), row(norm2_g[l]),
               w_up[l].astype(BF16), w_down[l].astype(BF16), row(normf_g))

    y_prompt, pool_tail, h_fin = _prompt_call(x_prompt, weights)
    pool_width = pool_tail.shape[-1]
    pool_p = pool_tail.reshape(HALO_STEPS, batch, pool_width)[HALO_STEPS - POOL_BUF:].transpose(1, 0, 2)
    h_fin = h_fin.reshape(batch, n_states // SLAB_STATES, 2, SLAB_STATES)
    re_p = h_fin[:, :, 0].reshape(batch, groups, n_state)
    im_p = h_fin[:, :, 1].reshape(batch, groups, n_state)

    cache_tm = cache_pool[l].transpose(1, 0, 2)
    y_s, v_s, re_s, im_s = _sample_call(
        x_sample.reshape(dec_batch, d), cache_tm,
        state_ssm_re[l].reshape(dec_batch, n_states), state_ssm_im[l].reshape(dec_batch, n_states),
        weights)
    pool_s = jnp.concatenate([cache_pool[l][:, 1:], v_s[:, None, :]], axis=1)

    return (y_prompt, y_s.reshape(dec_batch, dec_seq, d),
            pool_p[None], re_p[None], im_p[None],
            pool_s[None], re_s.reshape(dec_batch, groups, n_state)[None],
            im_s.reshape(dec_batch, groups, n_state)[None])
```

```python
import functools
import math

import jax
import jax.numpy as jnp
from jax import lax
from jax.experimental import pallas as pl
from jax.experimental.pallas import tpu as pltpu

POOL_WINDOWS = (2, 4, 8, 16)
POOL_BUF = max(POOL_WINDOWS) - 1
SSM_GROUP_W = 16
SSM_STATE = 64
RMS_EPS = 1e-6
PAST_LEN = 16384

LANES = 128
SUBLANES = 8
GROUPS_PER_SLAB = LANES // SSM_GROUP_W
SLAB_STATES = GROUPS_PER_SLAB * SSM_STATE
STEPS_PER_TILE = 32
HALO_STEPS = 16
X_SLOTS = 3
FF_CHUNK = 1024
W_CHUNK_COLS = 1024
VMEM_LIMIT_BYTES = 56 * 1024 * 1024

F32 = jnp.float32
BF16 = jnp.bfloat16


def _dot(a, b):
    return jnp.dot(a, b, preferred_element_type=F32)


def _sigmoid(x):
    return 1.0 / (1.0 + jnp.exp(-x))


def _rmsnorm(x, g):
    y = x * lax.rsqrt(jnp.mean(x * x, axis=-1, keepdims=True) + RMS_EPS)
    return y * g


def _full_spec(shape):
    return pl.BlockSpec(shape, lambda i: (0,) * len(shape))


def _discretize(a_re, a_im, dt):
    mag = jnp.exp(a_re * dt)
    ang = a_im * dt
    lam_re = mag * jnp.cos(ang)
    lam_im = mag * jnp.sin(ang)
    p = lam_re - 1.0
    q = lam_im
    den = a_re * a_re + a_im * a_im
    coef_re = (p * a_re + q * a_im) / den
    coef_im = (q * a_re - p * a_im) / den
    return lam_re, lam_im, coef_re, coef_im


def _ssm_prep_kernel(a_re_ref, a_im_ref, a_re_rep_ref, a_im_rep_ref, log_dt_ref,
                     b_re_ref, b_im_ref,
                     lam_re_ref, lam_im_ref, bbar_re_ref, bbar_im_ref):
    dt = jnp.exp(log_dt_ref[...])
    lam_re, lam_im, _, _ = _discretize(a_re_ref[...], a_im_ref[...], dt)
    lam_re_ref[...] = lam_re
    lam_im_ref[...] = lam_im
    _, _, coef_re, coef_im = _discretize(a_re_rep_ref[...], a_im_rep_ref[...], dt)
    b_re = b_re_ref[...]
    b_im = b_im_ref[...]
    bbar_re_ref[...] = coef_re * b_re - coef_im * b_im
    bbar_im_ref[...] = coef_re * b_im + coef_im * b_re


def _ssm_prep(a_re, a_im, log_dt, b_re, b_im):
    g, n = a_re.shape
    c = b_re.shape[-1]
    ins = (a_re, a_im, jnp.repeat(a_re, c, axis=1), jnp.repeat(a_im, c, axis=1),
           log_dt.reshape(g, 1), b_re.reshape(g, n * c), b_im.reshape(g, n * c))
    outs = (jax.ShapeDtypeStruct((g, n), F32), jax.ShapeDtypeStruct((g, n), F32),
            jax.ShapeDtypeStruct((g, n * c), F32), jax.ShapeDtypeStruct((g, n * c), F32))
    lam_re, lam_im, bbar_re, bbar_im = pl.pallas_call(
        _ssm_prep_kernel,
        out_shape=outs,
        grid=(1,),
        in_specs=[_full_spec(x.shape) for x in ins],
        out_specs=[_full_spec(o.shape) for o in outs],
        name="ssm_prep",
    )(*ins)
    return lam_re, lam_im, bbar_re.reshape(g, n, c), bbar_im.reshape(g, n, c)


def _block_diag_ssm(bbar_re, bbar_im, c_re, c_im):
    g, n, c = bbar_re.shape
    slabs = g // GROUPS_PER_SLAB
    eye = jnp.eye(GROUPS_PER_SLAB, dtype=F32)

    def b_blocks(b):
        b = b.reshape(slabs, GROUPS_PER_SLAB, n, c)
        return jnp.einsum('sgnc,gh->sgchn', b, eye).reshape(slabs, LANES, SLAB_STATES)

    def c_blocks(m):
        m = m.reshape(slabs, GROUPS_PER_SLAB, c, n)
        return jnp.einsum('sgcn,gh->sgnhc', m, eye).reshape(slabs, SLAB_STATES, LANES)

    b_cat = jnp.concatenate([b_blocks(bbar_re), b_blocks(bbar_im)], axis=2)
    c_cat = jnp.concatenate([c_blocks(c_re), -c_blocks(c_im)], axis=1)
    return b_cat.astype(BF16), c_cat.astype(BF16)


def _pool_hist_kernel(cache_ref, hist_ref):
    for g, w in enumerate(POOL_WINDOWS):
        cols = slice(g * LANES, (g + 1) * LANES)
        acc = cache_ref[:, POOL_BUF - 1, cols]
        for s in range(2, w):
            acc = acc + cache_ref[:, POOL_BUF - s, cols]
        hist_ref[:, cols] = acc


def _pool_hist(cache):
    n, _, width = cache.shape
    return pl.pallas_call(
        _pool_hist_kernel,
        out_shape=jax.ShapeDtypeStruct((n, width), F32),
        grid=(1,),
        in_specs=[_full_spec(cache.shape)],
        out_specs=_full_spec((n, width)),
        name="pool_hist",
    )(cache)


def _pool_project(pooled, pool_w_ref, pool_scale_ref, pool_out_ref):
    mixed = [_dot(p.astype(BF16), pool_w_ref[g * LANES:(g + 1) * LANES, :])
             for g, p in enumerate(pooled)]
    yp = jnp.concatenate(mixed, axis=-1) * pool_scale_ref[...]
    return _dot(yp.astype(BF16), pool_out_ref[...])


def _finish_layer(x, h1, branch_pool, ys, w_in_ref, glu_ref, w_out_ref, g2_ref,
                  w_up_ref, w_down_ref):
    d = x.shape[-1]
    z = _dot(ys.astype(BF16), glu_ref[...])
    branch_ssm = z[:, :d] * _sigmoid(z[:, d:])
    gate_off = w_in_ref.shape[1] - 2 * d
    g_pool = _sigmoid(_dot(h1, w_in_ref[:, gate_off:gate_off + d]))
    g_ssm = _sigmoid(_dot(h1, w_in_ref[:, gate_off + d:]))
    merged = g_pool * branch_pool + g_ssm * branch_ssm
    x1 = x + _dot(merged.astype(BF16), w_out_ref[...])
    h2 = _rmsnorm(x1, g2_ref[...]).astype(BF16)
    d_ff = w_up_ref.shape[1]
    acc = None
    for c0 in range(0, d_ff, FF_CHUNK):
        a = jnp.maximum(_dot(h2, w_up_ref[:, c0:c0 + FF_CHUNK]), 0.0)
        part = _dot((a * a).astype(BF16), w_down_ref[c0:c0 + FF_CHUNK, :])
        acc = part if acc is None else acc + part
    return x1 + acc


def _load_weights_as_bf16(pairs, stage, sem):
    rows, stage_cols = stage.shape
    ring = stage_cols // W_CHUNK_COLS
    chunks = []
    for w_hbm, w_vm in pairs:
        k, n = w_hbm.shape
        cols = min(n, W_CHUNK_COLS)
        for r0 in range(0, k, rows):
            for c0 in range(0, n, cols):
                chunks.append((w_hbm, w_vm, r0, c0, cols))

    def copy(idx):
        w_hbm, _, r0, c0, cols = chunks[idx]
        sl = idx % ring
        return pltpu.make_async_copy(
            w_hbm.at[pl.ds(r0, rows), pl.ds(c0, cols)],
            stage.at[:, pl.ds(sl * W_CHUNK_COLS, cols)], sem.at[sl])

    for idx in range(min(ring, len(chunks))):
        copy(idx).start()
    for idx, (_, w_vm, r0, c0, cols) in enumerate(chunks):
        sl = idx % ring
        copy(idx).wait()
        w_vm[r0:r0 + rows, c0:c0 + cols] = stage[:, sl * W_CHUNK_COLS:sl * W_CHUNK_COLS + cols].astype(BF16)
        if idx + ring < len(chunks):
            copy(idx + ring).start()


def _layer_kernel(n_tiles,
                  x_hbm, w_in_hbm, pool_w_hbm, pool_out_hbm, glu_hbm, w_out_hbm, w_up_hbm, w_down_hbm,
                  g1_ref, pool_scale_ref, lam_re_ref, lam_im_ref, b_cat_ref, c_cat_ref, d_skip_ref,
                  g2_ref, gf_ref,
                  xs_ref, hist_ref, h0_re_ref, h0_im_ref,
                  y_hbm, pool_tail_ref, h_fin_ref, ys_out_ref, vs_out_ref, hs_re_ref, hs_im_ref,
                  xbuf, ybuf, sem_in, sem_out, sem_w, h1buf, x2buf, vbuf, s_buf, h_state,
                  w_in_ref, pool_w_ref, pool_out_ref, glu_ref, w_out_ref, w_up_ref, w_down_ref):
    i = pl.program_id(0)
    n_x, steps, batch, d_model = xbuf.shape
    rows = steps * batch
    halo = HALO_STEPS * batch
    n_slabs = b_cat_ref.shape[0]
    pool_width = vbuf.shape[1]

    def in_copy(tile, b):
        sl = lax.rem(tile, n_x)
        return pltpu.make_async_copy(x_hbm.at[b, pl.ds(tile * steps, steps), :],
                                     xbuf.at[sl, :, b, :], sem_in.at[sl, b])

    def out_copy(tile, b):
        sl = lax.rem(tile, 2)
        return pltpu.make_async_copy(ybuf.at[sl, :, b, :],
                                     y_hbm.at[b, pl.ds(tile * steps, steps), :], sem_out.at[sl, b])

    def final_norm_to_ybuf(sl):
        y = _rmsnorm(x2buf[sl], gf_ref[...])
        ybuf[sl] = y.reshape(steps, batch, d_model)

    def ssm_cols(j):
        re_cols = slice(2 * SLAB_STATES * j, 2 * SLAB_STATES * j + SLAB_STATES)
        im_cols = slice(2 * SLAB_STATES * j + SLAB_STATES, 2 * SLAB_STATES * (j + 1))
        return slice(SLAB_STATES * j, SLAB_STATES * (j + 1)), re_cols, im_cols

    @pl.when(i == 0)
    def _():
        for b in range(batch):
            in_copy(0, b).start()
        for b in range(batch):
            in_copy(1, b).start()
        _load_weights_as_bf16(
            [(w_in_hbm, w_in_ref), (pool_w_hbm, pool_w_ref), (pool_out_hbm, pool_out_ref),
             (glu_hbm, glu_ref), (w_out_hbm, w_out_ref), (w_up_hbm, w_up_ref),
             (w_down_hbm, w_down_ref)], s_buf, sem_w)
        vbuf[0:halo, :] = jnp.zeros((halo, pool_width), F32)
        h_state[...] = jnp.zeros(h_state.shape, F32)
        x2buf[1] = jnp.zeros(x2buf.shape[1:], F32)
        for b in range(batch):
            in_copy(0, b).wait()
        h1buf[0] = _rmsnorm(xbuf[0].reshape(rows, d_model), g1_ref[...]).astype(BF16)

    @pl.when(i < n_tiles)
    def _prompt_step():
        xslot = lax.rem(i, n_x)
        xslot_next = lax.rem(i + 1, n_x)
        par = lax.rem(i, 2)

        @pl.when(i + 2 < n_tiles)
        def _():
            for b in range(batch):
                in_copy(i + 2, b).start()

        @pl.when(i + 1 < n_tiles)
        def _():
            for b in range(batch):
                in_copy(i + 1, b).wait()

        @pl.when(i >= 3)
        def _():
            for b in range(batch):
                out_copy(i - 3, b).wait()

        final_norm_to_ybuf(1 - par)

        x = xbuf[xslot].reshape(rows, d_model)
        h1 = h1buf[par]
        v = _dot(h1, w_in_ref[:, 0:pool_width])
        vbuf[halo:halo + rows, :] = v
        u = _dot(h1, w_in_ref[:, pool_width:2 * pool_width])

        row = lax.broadcasted_iota(jnp.int32, (rows, LANES), 0)
        pos = i * steps + lax.shift_right_logical(row, int(math.log2(batch))) + 1
        pooled = []
        for g, w in enumerate(POOL_WINDOWS):
            cols = slice(g * LANES, (g + 1) * LANES)
            cur = vbuf[halo:halo + rows, cols]
            win = cur
            for s in range(1, w):
                win = win + vbuf[halo - s * batch:halo - s * batch + rows, cols]
            cnt = jnp.minimum(w, pos).astype(F32)
            pooled.append(win / cnt - cur)
        branch_pool = _pool_project(pooled, pool_w_ref, pool_scale_ref, pool_out_ref)

        u_bf = u.astype(BF16)
        for j in range(n_slabs):
            s_buf[:, 2 * SLAB_STATES * j:2 * SLAB_STATES * (j + 1)] = _dot(
                u_bf[:, LANES * j:LANES * (j + 1)], b_cat_ref[j])
        for j in range(n_slabs):
            st, re_cols, im_cols = ssm_cols(j)
            lr = jnp.broadcast_to(lam_re_ref[:, st], (batch, SLAB_STATES))
            li = jnp.broadcast_to(lam_im_ref[:, st], (batch, SLAB_STATES))
            hr = h_state[:, re_cols]
            hi = h_state[:, im_cols]
            for t in range(steps):
                r = slice(t * batch, (t + 1) * batch)
                hr, hi = (lr * hr - li * hi + s_buf[r, re_cols],
                          lr * hi + li * hr + s_buf[r, im_cols])
                s_buf[r, re_cols] = hr
                s_buf[r, im_cols] = hi
            h_state[:, re_cols] = hr
            h_state[:, im_cols] = hi
        ys = jnp.concatenate(
            [_dot(s_buf[:, 2 * SLAB_STATES * j:2 * SLAB_STATES * (j + 1)].astype(BF16), c_cat_ref[j])
             for j in range(n_slabs)], axis=-1)
        ys = ys + d_skip_ref[...] * u

        x2buf[par] = _finish_layer(x, h1, branch_pool, ys, w_in_ref, glu_ref, w_out_ref, g2_ref,
                                   w_up_ref, w_down_ref)

        tail = vbuf[rows:rows + halo, :]
        vbuf[0:halo, :] = tail
        pool_tail_ref[...] = tail
        h_fin_ref[...] = h_state[...]

        h1buf[1 - par] = _rmsnorm(xbuf[xslot_next].reshape(rows, d_model), g1_ref[...]).astype(BF16)

        @pl.when(i >= 1)
        def _():
            for b in range(batch):
                out_copy(i - 1, b).start()

        @pl.when(i == n_tiles - 1)
        def _():
            for b in range(batch):
                out_copy(i - 2, b).wait()
            final_norm_to_ybuf(par)
            for b in range(batch):
                out_copy(i, b).start()
            for b in range(batch):
                out_copy(i - 1, b).wait()
            for b in range(batch):
                out_copy(i, b).wait()

    @pl.when(i == n_tiles)
    def _decode_step():
        x = xs_ref[...]
        h1 = _rmsnorm(x, g1_ref[...]).astype(BF16)
        v = _dot(h1, w_in_ref[:, 0:pool_width])
        vs_out_ref[...] = v
        u = _dot(h1, w_in_ref[:, pool_width:2 * pool_width])

        pooled = []
        for g, w in enumerate(POOL_WINDOWS):
            cols = slice(g * LANES, (g + 1) * LANES)
            cur = v[:, cols]
            cnt = float(min(w, PAST_LEN + 1))
            pooled.append((hist_ref[:, cols] + cur) / cnt - cur)
        branch_pool = _pool_project(pooled, pool_w_ref, pool_scale_ref, pool_out_ref)

        u_bf = u.astype(BF16)
        ys_parts = []
        for j in range(n_slabs):
            st, _, _ = ssm_cols(j)
            bu = _dot(u_bf[:, LANES * j:LANES * (j + 1)], b_cat_ref[j])
            lr = lam_re_ref[:, st]
            li = lam_im_ref[:, st]
            hr = h0_re_ref[:, st]
            hi = h0_im_ref[:, st]
            nr = lr * hr - li * hi + bu[:, :SLAB_STATES]
            ni = lr * hi + li * hr + bu[:, SLAB_STATES:]
            hs_re_ref[:, st] = nr
            hs_im_ref[:, st] = ni
            hcat = jnp.concatenate([nr, ni], axis=-1).astype(BF16)
            ys_parts.append(_dot(hcat, c_cat_ref[j]))
        ys = jnp.concatenate(ys_parts, axis=-1) + d_skip_ref[...] * u

        x2 = _finish_layer(x, h1, branch_pool, ys, w_in_ref, glu_ref, w_out_ref, g2_ref,
                           w_up_ref, w_down_ref)
        ys_out_ref[...] = _rmsnorm(x2, gf_ref[...])


def _const_spec(shape):
    return pl.BlockSpec(shape, lambda i: (0,) * len(shape), pipeline_mode=pl.Buffered(1))


def _layer_call(x, big_weights, small_params, sample_ins):
    batch, seq, d = x.shape
    steps = STEPS_PER_TILE
    rows = steps * batch
    n_tiles = seq // steps
    assert n_tiles >= X_SLOTS, "the input ring and the deferred final norm assume at least 3 tiles"
    xs, hist, h0_re, _ = sample_ins
    dec_rows = xs.shape[0]
    pool_width = hist.shape[-1]
    n_states = h0_re.shape[-1]
    halo = HALO_STEPS * batch
    assert all(w.shape[0] % rows == 0 and w.shape[1] % min(w.shape[1], W_CHUNK_COLS) == 0
               for w in big_weights), "weight staging chunks must tile every weight"
    out_shape = (jax.ShapeDtypeStruct((batch, seq, d), F32),
                 jax.ShapeDtypeStruct((halo, pool_width), F32),
                 jax.ShapeDtypeStruct((batch, 2 * n_states), F32),
                 jax.ShapeDtypeStruct((dec_rows, d), F32),
                 jax.ShapeDtypeStruct((dec_rows, pool_width), F32),
                 jax.ShapeDtypeStruct((dec_rows, n_states), F32),
                 jax.ShapeDtypeStruct((dec_rows, n_states), F32))
    any_spec = pl.BlockSpec(memory_space=pl.ANY)
    return pl.pallas_call(
        functools.partial(_layer_kernel, n_tiles),
        out_shape=out_shape,
        grid=(n_tiles + 1,),
        in_specs=([any_spec] * (1 + len(big_weights))
                  + [_const_spec(p.shape) for p in small_params]
                  + [_const_spec(a.shape) for a in sample_ins]),
        out_specs=[any_spec] + [_full_spec(o.shape) for o in out_shape[1:]],
        scratch_shapes=[pltpu.VMEM((X_SLOTS, steps, batch, d), F32),
                        pltpu.VMEM((2, steps, batch, d), F32),
                        pltpu.SemaphoreType.DMA((X_SLOTS, batch)),
                        pltpu.SemaphoreType.DMA((2, batch)),
                        pltpu.SemaphoreType.DMA((2 * n_states // W_CHUNK_COLS,)),
                        pltpu.VMEM((2, rows, d), BF16),
                        pltpu.VMEM((2, rows, d), F32),
                        pltpu.VMEM((halo + rows, pool_width), F32),
                        pltpu.VMEM((rows, 2 * n_states), F32),
                        pltpu.VMEM((batch, 2 * n_states), F32)]
                       + [pltpu.VMEM(w.shape, BF16) for w in big_weights],
        compiler_params=pltpu.CompilerParams(dimension_semantics=("arbitrary",),
                                             vmem_limit_bytes=VMEM_LIMIT_BYTES),
        name="layer",
    )(x, *big_weights, *small_params, *sample_ins)


def kernel(x_prompt, x_sample, cache_pool, state_ssm_re, state_ssm_im, norm1_g, w_in, pool_w,
           pool_scale, pool_out, ssm_a_re, ssm_a_im, ssm_log_dt, ssm_b_re, ssm_b_im, ssm_c_re,
           ssm_c_im, ssm_d, ssm_glu, w_out, norm2_g, w_up, w_down, normf_g):
    depth = w_in.shape[0]
    assert depth == 1, "the final norm is fused into the (single) layer kernel"
    batch, seq, d = x_prompt.shape
    dec_batch, dec_seq, _ = x_sample.shape
    assert dec_seq == 1 and batch == SUBLANES and seq % STEPS_PER_TILE == 0
    groups, n_state = ssm_a_re.shape[1:]
    n_states = groups * n_state
    l = 0

    lam_re, lam_im, bbar_re, bbar_im = _ssm_prep(ssm_a_re[l], ssm_a_im[l], ssm_log_dt[l],
                                                 ssm_b_re[l], ssm_b_im[l])
    b_cat, c_cat = _block_diag_ssm(bbar_re, bbar_im, ssm_c_re[l], ssm_c_im[l])
    row = lambda a: a.reshape(1, -1)
    pool_groups, pool_gw, _ = pool_w[l].shape
    big_weights = (w_in[l], pool_w[l].reshape(pool_groups * pool_gw, pool_gw), pool_out[l],
                   ssm_glu[l], w_out[l], w_up[l], w_down[l])
    small_params = (row(norm1_g[l]), row(pool_scale[l]), row(lam_re), row(lam_im), b_cat, c_cat,
                    row(ssm_d[l]), row(norm2_g[l]), row(normf_g))
    sample_ins = (x_sample.reshape(dec_batch, d), _pool_hist(cache_pool[l]),
                  state_ssm_re[l].reshape(dec_batch, n_states),
                  state_ssm_im[l].reshape(dec_batch, n_states))

    y_prompt, pool_tail, h_fin, y_s, v_s, re_s, im_s = _layer_call(
        x_prompt, big_weights, small_params, sample_ins)

    pool_width = pool_tail.shape[-1]
    pool_p = pool_tail.reshape(HALO_STEPS, batch, pool_width)[HALO_STEPS - POOL_BUF:].transpose(1, 0, 2)
    h_fin = h_fin.reshape(batch, n_states // SLAB_STATES, 2, SLAB_STATES)
    re_p = h_fin[:, :, 0].reshape(batch, groups, n_state)
    im_p = h_fin[:, :, 1].reshape(batch, groups, n_state)
    pool_s = jnp.concatenate([cache_pool[l][:, 1:], v_s[:, None, :]], axis=1)

    return (y_prompt, y_s.reshape(dec_batch, dec_seq, d),
            pool_p[None], re_p[None], im_p[None],
            pool_s[None], re_s.reshape(dec_batch, groups, n_state)[None],
            im_s.reshape(dec_batch, groups, n_state)[None])
```

```python
import functools
import math

import jax
import jax.numpy as jnp
from jax import lax
from jax.experimental import pallas as pl
from jax.experimental.pallas import tpu as pltpu

POOL_WINDOWS = (2, 4, 8, 16)
POOL_BUF = max(POOL_WINDOWS) - 1
SSM_GROUP_W = 16
SSM_STATE = 64
RMS_EPS = 1e-6
PAST_LEN = 16384

LANES = 128
SUBLANES = 8
GROUPS_PER_SLAB = LANES // SSM_GROUP_W
SLAB_STATES = GROUPS_PER_SLAB * SSM_STATE
STEPS_PER_TILE = 32
HALO_STEPS = 16
X_SLOTS = 3
FF_CHUNK = 1024
W_CHUNK_COLS = 1024
VMEM_LIMIT_BYTES = 56 * 1024 * 1024

F32 = jnp.float32
BF16 = jnp.bfloat16


def _dot(a, b):
    return jnp.dot(a, b, preferred_element_type=F32)


def _sigmoid(x):
    return 1.0 / (1.0 + jnp.exp(-x))


def _rmsnorm(x, g):
    y = x * lax.rsqrt(jnp.mean(x * x, axis=-1, keepdims=True) + RMS_EPS)
    return y * g


def _full_spec(shape):
    return pl.BlockSpec(shape, lambda i: (0,) * len(shape))


def _discretize(a_re, a_im, dt):
    mag = jnp.exp(a_re * dt)
    ang = a_im * dt
    lam_re = mag * jnp.cos(ang)
    lam_im = mag * jnp.sin(ang)
    p = lam_re - 1.0
    q = lam_im
    den = a_re * a_re + a_im * a_im
    coef_re = (p * a_re + q * a_im) / den
    coef_im = (q * a_re - p * a_im) / den
    return lam_re, lam_im, coef_re, coef_im


def _tile_states(x):
    x2 = jnp.concatenate([x, x], axis=-1)
    return jnp.concatenate([x2] * (SLAB_STATES // (2 * SSM_STATE)), axis=-1)


def _ssm_prep_kernel(a_re_ref, a_im_ref, log_dt_ref, b_re_ref, b_im_ref, c_re_ref, c_im_ref,
                     lam_re_ref, lam_im_ref, b_cat_ref, c_cat_ref):
    groups, n = a_re_ref.shape
    slabs = b_cat_ref.shape[0]
    chans = b_re_ref.shape[0] // groups
    dt = jnp.exp(log_dt_ref[...])
    a_re = a_re_ref[...]
    a_im = a_im_ref[...]

    lr, li, _, _ = _discretize(_tile_states(a_re), _tile_states(a_im), dt)
    row = lax.broadcasted_iota(jnp.int32, (groups, SLAB_STATES), 0)
    col = lax.broadcasted_iota(jnp.int32, (groups, SLAB_STATES), 1)
    own = (row % GROUPS_PER_SLAB) == (col // n)
    lam_re_ref[...] = jnp.where(own, lr, 0.0).reshape(slabs, GROUPS_PER_SLAB, SLAB_STATES).sum(axis=1)
    lam_im_ref[...] = jnp.where(own, li, 0.0).reshape(slabs, GROUPS_PER_SLAB, SLAB_STATES).sum(axis=1)

    rep = lambda a: jnp.broadcast_to(a[:, None, :], (groups, chans, a.shape[-1])).reshape(
        groups * chans, a.shape[-1])
    _, _, coef_re, coef_im = _discretize(rep(a_re), rep(a_im), rep(dt))
    b_re = b_re_ref[...]
    b_im = b_im_ref[...]
    bbar_re = coef_re * b_re - coef_im * b_im
    bbar_im = coef_re * b_im + coef_im * b_re

    r2 = lax.broadcasted_iota(jnp.int32, (LANES, SLAB_STATES), 0)
    c2 = lax.broadcasted_iota(jnp.int32, (LANES, SLAB_STATES), 1)
    diag = (r2 // chans) == (c2 // n)
    block_diag = lambda m: jnp.where(diag, _tile_states(m), 0.0)
    for j in range(slabs):
        rows = slice(LANES * j, LANES * (j + 1))
        b_cat_ref[j] = jnp.concatenate(
            [block_diag(bbar_re[rows]), block_diag(bbar_im[rows])], axis=-1).astype(BF16)
        c_cat_ref[j] = jnp.concatenate(
            [block_diag(c_re_ref[rows, :]).T, -block_diag(c_im_ref[rows, :]).T], axis=0).astype(BF16)


def _ssm_prep(a_re, a_im, log_dt, b_re, b_im, c_re, c_im):
    g, n = a_re.shape
    c = b_re.shape[-1]
    slabs = g // GROUPS_PER_SLAB
    ins = (a_re, a_im, log_dt.reshape(g, 1),
           b_re.transpose(0, 2, 1).reshape(g * c, n), b_im.transpose(0, 2, 1).reshape(g * c, n),
           c_re.reshape(g * c, n), c_im.reshape(g * c, n))
    outs = (jax.ShapeDtypeStruct((slabs, SLAB_STATES), F32),
            jax.ShapeDtypeStruct((slabs, SLAB_STATES), F32),
            jax.ShapeDtypeStruct((slabs, LANES, 2 * SLAB_STATES), BF16),
            jax.ShapeDtypeStruct((slabs, 2 * SLAB_STATES, LANES), BF16))
    return pl.pallas_call(
        _ssm_prep_kernel,
        out_shape=outs,
        grid=(1,),
        in_specs=[_full_spec(x.shape) for x in ins],
        out_specs=[_full_spec(o.shape) for o in outs],
        name="ssm_prep",
    )(*ins)


def _pool_hist_kernel(cache_ref, hist_ref):
    for g, w in enumerate(POOL_WINDOWS):
        cols = slice(g * LANES, (g + 1) * LANES)
        acc = cache_ref[POOL_BUF - 1, :, cols]
        for s in range(2, w):
            acc = acc + cache_ref[POOL_BUF - s, :, cols]
        hist_ref[:, cols] = acc


def _pool_hist(cache_tm):
    _, n, width = cache_tm.shape
    return pl.pallas_call(
        _pool_hist_kernel,
        out_shape=jax.ShapeDtypeStruct((n, width), F32),
        grid=(1,),
        in_specs=[_full_spec(cache_tm.shape)],
        out_specs=_full_spec((n, width)),
        name="pool_hist",
    )(cache_tm)


def _pool_project(pooled, pool_w_ref, pool_scale_ref, pool_out_ref):
    mixed = [_dot(p.astype(BF16), pool_w_ref[g * LANES:(g + 1) * LANES, :])
             for g, p in enumerate(pooled)]
    yp = jnp.concatenate(mixed, axis=-1) * pool_scale_ref[...]
    return _dot(yp.astype(BF16), pool_out_ref[...])


def _finish_layer(x, h1, branch_pool, ys, w_in_ref, glu_ref, w_out_ref, g2_ref,
                  w_up_ref, w_down_ref):
    d = x.shape[-1]
    z = _dot(ys.astype(BF16), glu_ref[...])
    branch_ssm = z[:, :d] * _sigmoid(z[:, d:])
    gate_off = w_in_ref.shape[1] - 2 * d
    g_pool = _sigmoid(_dot(h1, w_in_ref[:, gate_off:gate_off + d]))
    g_ssm = _sigmoid(_dot(h1, w_in_ref[:, gate_off + d:]))
    merged = g_pool * branch_pool + g_ssm * branch_ssm
    x1 = x + _dot(merged.astype(BF16), w_out_ref[...])
    h2 = _rmsnorm(x1, g2_ref[...]).astype(BF16)
    d_ff = w_up_ref.shape[1]
    acc = None
    for c0 in range(0, d_ff, FF_CHUNK):
        a = jnp.maximum(_dot(h2, w_up_ref[:, c0:c0 + FF_CHUNK]), 0.0)
        part = _dot((a * a).astype(BF16), w_down_ref[c0:c0 + FF_CHUNK, :])
        acc = part if acc is None else acc + part
    return x1 + acc


def _load_weights_as_bf16(pairs, stage, sem):
    rows, stage_cols = stage.shape
    ring = stage_cols // W_CHUNK_COLS
    chunks = []
    for w_hbm, w_vm in pairs:
        k, n = w_hbm.shape
        cols = min(n, W_CHUNK_COLS)
        for r0 in range(0, k, rows):
            for c0 in range(0, n, cols):
                chunks.append((w_hbm, w_vm, r0, c0, cols))

    def copy(idx):
        w_hbm, _, r0, c0, cols = chunks[idx]
        sl = idx % ring
        return pltpu.make_async_copy(
            w_hbm.at[pl.ds(r0, rows), pl.ds(c0, cols)],
            stage.at[:, pl.ds(sl * W_CHUNK_COLS, cols)], sem.at[sl])

    for idx in range(min(ring, len(chunks))):
        copy(idx).start()
    for idx, (_, w_vm, r0, c0, cols) in enumerate(chunks):
        sl = idx % ring
        copy(idx).wait()
        w_vm[r0:r0 + rows, c0:c0 + cols] = stage[:, sl * W_CHUNK_COLS:sl * W_CHUNK_COLS + cols].astype(BF16)
        if idx + ring < len(chunks):
            copy(idx + ring).start()


def _layer_kernel(n_tiles,
                  x_hbm, w_in_hbm, pool_w_hbm, pool_out_hbm, glu_hbm, w_out_hbm, w_up_hbm, w_down_hbm,
                  g1_ref, pool_scale_ref, lam_re_ref, lam_im_ref, b_cat_ref, c_cat_ref, d_skip_ref,
                  g2_ref, gf_ref,
                  xs_ref, hist_ref, h0t_re_ref, h0t_im_ref, cache_hbm,
                  y_hbm, pool_s_hbm, pool_tail_ref, h_fin_ref, ys_out_ref, hst_re_ref, hst_im_ref,
                  xbuf, ybuf, sem_in, sem_out, sem_w, sem_pool, h1buf, x2buf, vbuf, s_buf, h_state, vs_buf,
                  w_in_ref, pool_w_ref, pool_out_ref, glu_ref, w_out_ref, w_up_ref, w_down_ref):
    i = pl.program_id(0)
    n_x, steps, batch, d_model = xbuf.shape
    rows = steps * batch
    halo = HALO_STEPS * batch
    n_slabs = b_cat_ref.shape[0]
    pool_width = vbuf.shape[1]

    def in_copy(tile, b):
        sl = lax.rem(tile, n_x)
        return pltpu.make_async_copy(x_hbm.at[b, pl.ds(tile * steps, steps), :],
                                     xbuf.at[sl, :, b, :], sem_in.at[sl, b])

    def out_copy(tile, b):
        sl = lax.rem(tile, 2)
        return pltpu.make_async_copy(ybuf.at[sl, :, b, :],
                                     y_hbm.at[b, pl.ds(tile * steps, steps), :], sem_out.at[sl, b])

    def final_norm_to_ybuf(sl):
        y = _rmsnorm(x2buf[sl], gf_ref[...])
        ybuf[sl] = y.reshape(steps, batch, d_model)

    def cache_shift_copy():
        return pltpu.make_async_copy(cache_hbm.at[pl.ds(1, POOL_BUF - 1)],
                                     pool_s_hbm.at[pl.ds(0, POOL_BUF - 1)], sem_pool.at[0])

    def cache_new_row_copy():
        return pltpu.make_async_copy(vs_buf, pool_s_hbm.at[POOL_BUF - 1], sem_pool.at[1])

    def ssm_cols(j):
        re_cols = slice(2 * SLAB_STATES * j, 2 * SLAB_STATES * j + SLAB_STATES)
        im_cols = slice(2 * SLAB_STATES * j + SLAB_STATES, 2 * SLAB_STATES * (j + 1))
        return slice(SLAB_STATES * j, SLAB_STATES * (j + 1)), re_cols, im_cols

    @pl.when(i == 0)
    def _():
        for b in range(batch):
            in_copy(0, b).start()
        for b in range(batch):
            in_copy(1, b).start()
        cache_shift_copy().start()
        _load_weights_as_bf16(
            [(w_in_hbm, w_in_ref), (pool_w_hbm, pool_w_ref), (pool_out_hbm, pool_out_ref),
             (glu_hbm, glu_ref), (w_out_hbm, w_out_ref), (w_up_hbm, w_up_ref),
             (w_down_hbm, w_down_ref)], s_buf, sem_w)
        vbuf[0:halo, :] = jnp.zeros((halo, pool_width), F32)
        h_state[...] = jnp.zeros(h_state.shape, F32)
        x2buf[1] = jnp.zeros(x2buf.shape[1:], F32)
        for b in range(batch):
            in_copy(0, b).wait()
        h1buf[0] = _rmsnorm(xbuf[0].reshape(rows, d_model), g1_ref[...]).astype(BF16)

    @pl.when(i < n_tiles)
    def _prompt_step():
        xslot = lax.rem(i, n_x)
        xslot_next = lax.rem(i + 1, n_x)
        par = lax.rem(i, 2)

        @pl.when(i + 2 < n_tiles)
        def _():
            for b in range(batch):
                in_copy(i + 2, b).start()

        @pl.when(i + 1 < n_tiles)
        def _():
            for b in range(batch):
                in_copy(i + 1, b).wait()

        @pl.when(i >= 3)
        def _():
            for b in range(batch):
                out_copy(i - 3, b).wait()

        final_norm_to_ybuf(1 - par)

        x = xbuf[xslot].reshape(rows, d_model)
        h1 = h1buf[par]
        v = _dot(h1, w_in_ref[:, 0:pool_width])
        vbuf[halo:halo + rows, :] = v
        u = _dot(h1, w_in_ref[:, pool_width:2 * pool_width])

        row = lax.broadcasted_iota(jnp.int32, (rows, LANES), 0)
        pos = i * steps + lax.shift_right_logical(row, int(math.log2(batch))) + 1
        pooled = []
        for g, w in enumerate(POOL_WINDOWS):
            cols = slice(g * LANES, (g + 1) * LANES)
            cur = vbuf[halo:halo + rows, cols]
            win = cur
            for s in range(1, w):
                win = win + vbuf[halo - s * batch:halo - s * batch + rows, cols]
            cnt = jnp.minimum(w, pos).astype(F32)
            pooled.append(win / cnt - cur)
        branch_pool = _pool_project(pooled, pool_w_ref, pool_scale_ref, pool_out_ref)

        u_bf = u.astype(BF16)
        for j in range(n_slabs):
            s_buf[:, 2 * SLAB_STATES * j:2 * SLAB_STATES * (j + 1)] = _dot(
                u_bf[:, LANES * j:LANES * (j + 1)], b_cat_ref[j])
        for j in range(n_slabs):
            st, re_cols, im_cols = ssm_cols(j)
            lr = jnp.broadcast_to(lam_re_ref[j:j + 1, :], (batch, SLAB_STATES))
            li = jnp.broadcast_to(lam_im_ref[j:j + 1, :], (batch, SLAB_STATES))
            hr = h_state[:, re_cols]
            hi = h_state[:, im_cols]
            for t in range(steps):
                r = slice(t * batch, (t + 1) * batch)
                hr, hi = (lr * hr - li * hi + s_buf[r, re_cols],
                          lr * hi + li * hr + s_buf[r, im_cols])
                s_buf[r, re_cols] = hr
                s_buf[r, im_cols] = hi
            h_state[:, re_cols] = hr
            h_state[:, im_cols] = hi
        ys = jnp.concatenate(
            [_dot(s_buf[:, 2 * SLAB_STATES * j:2 * SLAB_STATES * (j + 1)].astype(BF16), c_cat_ref[j])
             for j in range(n_slabs)], axis=-1)
        ys = ys + d_skip_ref[...] * u

        x2buf[par] = _finish_layer(x, h1, branch_pool, ys, w_in_ref, glu_ref, w_out_ref, g2_ref,
                                   w_up_ref, w_down_ref)

        tail = vbuf[rows:rows + halo, :]
        vbuf[0:halo, :] = tail
        pool_tail_ref[...] = tail[halo - POOL_BUF * batch:, :]
        h_fin_ref[...] = h_state[...]

        h1buf[1 - par] = _rmsnorm(xbuf[xslot_next].reshape(rows, d_model), g1_ref[...]).astype(BF16)

        @pl.when(i >= 1)
        def _():
            for b in range(batch):
                out_copy(i - 1, b).start()

        @pl.when(i == n_tiles - 1)
        def _():
            for b in range(batch):
                out_copy(i - 2, b).wait()
            final_norm_to_ybuf(par)
            for b in range(batch):
                out_copy(i, b).start()
            for b in range(batch):
                out_copy(i - 1, b).wait()
            for b in range(batch):
                out_copy(i, b).wait()

    @pl.when(i == n_tiles)
    def _decode_step():
        x = xs_ref[...]
        h1 = _rmsnorm(x, g1_ref[...]).astype(BF16)
        v = _dot(h1, w_in_ref[:, 0:pool_width])
        vs_buf[...] = v
        cache_new_row_copy().start()
        u = _dot(h1, w_in_ref[:, pool_width:2 * pool_width])

        pooled = []
        for g, w in enumerate(POOL_WINDOWS):
            cols = slice(g * LANES, (g + 1) * LANES)
            cur = v[:, cols]
            cnt = float(min(w, PAST_LEN + 1))
            pooled.append((hist_ref[:, cols] + cur) / cnt - cur)
        branch_pool = _pool_project(pooled, pool_w_ref, pool_scale_ref, pool_out_ref)

        u_bf = u.astype(BF16)
        ys_parts = []
        for j in range(n_slabs):
            st, _, _ = ssm_cols(j)
            bu = _dot(u_bf[:, LANES * j:LANES * (j + 1)], b_cat_ref[j])
            lr = lam_re_ref[j:j + 1, :]
            li = lam_im_ref[j:j + 1, :]
            hr = h0t_re_ref[st, :].T
            hi = h0t_im_ref[st, :].T
            nr = lr * hr - li * hi + bu[:, :SLAB_STATES]
            ni = lr * hi + li * hr + bu[:, SLAB_STATES:]
            hst_re_ref[st, :] = nr.T
            hst_im_ref[st, :] = ni.T
            hcat = jnp.concatenate([nr, ni], axis=-1).astype(BF16)
            ys_parts.append(_dot(hcat, c_cat_ref[j]))
        ys = jnp.concatenate(ys_parts, axis=-1) + d_skip_ref[...] * u

        x2 = _finish_layer(x, h1, branch_pool, ys, w_in_ref, glu_ref, w_out_ref, g2_ref,
                           w_up_ref, w_down_ref)
        ys_out_ref[...] = _rmsnorm(x2, gf_ref[...])
        cache_shift_copy().wait()
        cache_new_row_copy().wait()


def _const_spec(shape):
    return pl.BlockSpec(shape, lambda i: (0,) * len(shape), pipeline_mode=pl.Buffered(1))


def _layer_call(x, big_weights, small_params, sample_ins):
    batch, seq, d = x.shape
    steps = STEPS_PER_TILE
    rows = steps * batch
    n_tiles = seq // steps
    assert n_tiles >= X_SLOTS, "the input ring and the deferred final norm assume at least 3 tiles"
    xs, hist, h0t_re, _, cache_tm = sample_ins
    dec_rows = xs.shape[0]
    pool_width = hist.shape[-1]
    n_states = h0t_re.shape[0]
    halo = HALO_STEPS * batch
    assert all(w.shape[0] % rows == 0 and w.shape[1] % min(w.shape[1], W_CHUNK_COLS) == 0
               for w in big_weights), "weight staging chunks must tile every weight"
    out_shape = (jax.ShapeDtypeStruct((batch, seq, d), F32),
                 jax.ShapeDtypeStruct(cache_tm.shape, F32),
                 jax.ShapeDtypeStruct((POOL_BUF * batch, pool_width), F32),
                 jax.ShapeDtypeStruct((batch, 2 * n_states), F32),
                 jax.ShapeDtypeStruct((dec_rows, d), F32),
                 jax.ShapeDtypeStruct((n_states, dec_rows), F32),
                 jax.ShapeDtypeStruct((n_states, dec_rows), F32))
    any_spec = pl.BlockSpec(memory_space=pl.ANY)
    return pl.pallas_call(
        functools.partial(_layer_kernel, n_tiles),
        out_shape=out_shape,
        grid=(n_tiles + 1,),
        in_specs=([any_spec] * (1 + len(big_weights))
                  + [_const_spec(p.shape) for p in small_params]
                  + [_const_spec(a.shape) for a in sample_ins[:-1]] + [any_spec]),
        out_specs=[any_spec, any_spec] + [_full_spec(o.shape) for o in out_shape[2:]],
        scratch_shapes=[pltpu.VMEM((X_SLOTS, steps, batch, d), F32),
                        pltpu.VMEM((2, steps, batch, d), F32),
                        pltpu.SemaphoreType.DMA((X_SLOTS, batch)),
                        pltpu.SemaphoreType.DMA((2, batch)),
                        pltpu.SemaphoreType.DMA((2 * n_states // W_CHUNK_COLS,)),
                        pltpu.SemaphoreType.DMA((2,)),
                        pltpu.VMEM((2, rows, d), BF16),
                        pltpu.VMEM((2, rows, d), F32),
                        pltpu.VMEM((halo + rows, pool_width), F32),
                        pltpu.VMEM((rows, 2 * n_states), F32),
                        pltpu.VMEM((batch, 2 * n_states), F32),
                        pltpu.VMEM((dec_rows, pool_width), F32)]
                       + [pltpu.VMEM(w.shape, BF16) for w in big_weights],
        compiler_params=pltpu.CompilerParams(dimension_semantics=("arbitrary",),
                                             vmem_limit_bytes=VMEM_LIMIT_BYTES),
        name="layer",
    )(x, *big_weights, *small_params, *sample_ins)


def kernel(x_prompt, x_sample, cache_pool, state_ssm_re, state_ssm_im, norm1_g, w_in, pool_w,
           pool_scale, pool_out, ssm_a_re, ssm_a_im, ssm_log_dt, ssm_b_re, ssm_b_im, ssm_c_re,
           ssm_c_im, ssm_d, ssm_glu, w_out, norm2_g, w_up, w_down, normf_g):
    depth = w_in.shape[0]
    assert depth == 1, "the final norm is fused into the (single) layer kernel"
    batch, seq, d = x_prompt.shape
    dec_batch, dec_seq, _ = x_sample.shape
    assert dec_seq == 1 and batch == SUBLANES and seq % STEPS_PER_TILE == 0
    groups, n_state = ssm_a_re.shape[1:]
    n_states = groups * n_state
    l = 0

    lam_re, lam_im, b_cat, c_cat = _ssm_prep(ssm_a_re[l], ssm_a_im[l], ssm_log_dt[l], ssm_b_re[l],
                                             ssm_b_im[l], ssm_c_re[l], ssm_c_im[l])
    row = lambda a: a.reshape(1, -1)
    pool_groups, pool_gw, _ = pool_w[l].shape
    big_weights = (w_in[l], pool_w[l].reshape(pool_groups * pool_gw, pool_gw), pool_out[l],
                   ssm_glu[l], w_out[l], w_up[l], w_down[l])
    small_params = (row(norm1_g[l]), row(pool_scale[l]), lam_re, lam_im, b_cat, c_cat,
                    row(ssm_d[l]), row(norm2_g[l]), row(normf_g))
    cache_tm = cache_pool[l].transpose(1, 0, 2)
    state_t = lambda s: s.transpose(1, 2, 0).reshape(n_states, dec_batch)
    sample_ins = (x_sample.reshape(dec_batch, d), _pool_hist(cache_tm),
                  state_t(state_ssm_re[l]), state_t(state_ssm_im[l]), cache_tm)

    y_prompt, pool_s_tm, pool_tail, h_fin, y_s, re_s_t, im_s_t = _layer_call(
        x_prompt, big_weights, small_params, sample_ins)

    pool_width = pool_tail.shape[-1]
    pool_p = pool_tail.reshape(POOL_BUF, batch, pool_width).transpose(1, 0, 2)
    h_fin = h_fin.reshape(batch, n_states // SLAB_STATES, 2, SLAB_STATES)
    re_p = h_fin[:, :, 0].reshape(batch, groups, n_state)
    im_p = h_fin[:, :, 1].reshape(batch, groups, n_state)
    pool_s = pool_s_tm.transpose(1, 0, 2)
    state_back = lambda s: s.reshape(groups, n_state, dec_batch).transpose(2, 0, 1)

    return (y_prompt, y_s.reshape(dec_batch, dec_seq, d),
            pool_p[None], re_p[None], im_p[None],
            pool_s[None], state_back(re_s_t)[None], state_back(im_s_t)[None])
```

```python
import functools
import math

import jax
import jax.numpy as jnp
from jax import lax
from jax.experimental import pallas as pl
from jax.experimental.pallas import tpu as pltpu

POOL_WINDOWS = (2, 4, 8, 16)
POOL_BUF = max(POOL_WINDOWS) - 1
SSM_GROUP_W = 16
SSM_STATE = 64
RMS_EPS = 1e-6
PAST_LEN = 16384

LANES = 128
SUBLANES = 8
GROUPS_PER_SLAB = LANES // SSM_GROUP_W
SLAB_STATES = GROUPS_PER_SLAB * SSM_STATE
STEPS_PER_TILE = 32
HALO_STEPS = 16
X_SLOTS = 3
FF_CHUNK = 1024
W_CHUNK_COLS = 1024
VMEM_LIMIT_BYTES = 56 * 1024 * 1024

F32 = jnp.float32
BF16 = jnp.bfloat16


def _dot(a, b):
    return jnp.dot(a, b, preferred_element_type=F32)


def _sigmoid(x):
    return 1.0 / (1.0 + jnp.exp(-x))


def _rmsnorm(x, g):
    y = x * lax.rsqrt(jnp.mean(x * x, axis=-1, keepdims=True) + RMS_EPS)
    return y * g


def _full_spec(shape):
    return pl.BlockSpec(shape, lambda i: (0,) * len(shape))


def _discretize(a_re, a_im, dt):
    mag = jnp.exp(a_re * dt)
    ang = a_im * dt
    lam_re = mag * jnp.cos(ang)
    lam_im = mag * jnp.sin(ang)
    p = lam_re - 1.0
    q = lam_im
    den = a_re * a_re + a_im * a_im
    coef_re = (p * a_re + q * a_im) / den
    coef_im = (q * a_re - p * a_im) / den
    return lam_re, lam_im, coef_re, coef_im


def _tile_states(x):
    x2 = jnp.concatenate([x, x], axis=-1)
    return jnp.concatenate([x2] * (SLAB_STATES // (2 * SSM_STATE)), axis=-1)


def _scale_state_cols(m, lr, li):
    re, im = m[:, :SLAB_STATES], m[:, SLAB_STATES:]
    return jnp.concatenate([lr * re - li * im, lr * im + li * re], axis=-1)


def _scale_state_cols_ct(ct, lr, li):
    re, m = ct[:, :SLAB_STATES], ct[:, SLAB_STATES:]
    return jnp.concatenate([lr * re + li * m, lr * m - li * re], axis=-1)


def _ssm_prep_kernel(a_re_ref, a_im_ref, log_dt_ref, b_re_ref, b_im_ref, c_re_ref, c_im_ref,
                     lam_ref, lam2_ref, bc_ref, cc_ref, d_ref):
    groups, n = a_re_ref.shape
    slabs = bc_ref.shape[0]
    chans = b_re_ref.shape[0] // groups
    dt = jnp.exp(log_dt_ref[...])
    a_re = a_re_ref[...]
    a_im = a_im_ref[...]

    lr, li, _, _ = _discretize(_tile_states(a_re), _tile_states(a_im), dt)
    row = lax.broadcasted_iota(jnp.int32, (groups, SLAB_STATES), 0)
    col = lax.broadcasted_iota(jnp.int32, (groups, SLAB_STATES), 1)
    own = (row % GROUPS_PER_SLAB) == (col // n)
    lr = jnp.where(own, lr, 0.0).reshape(slabs, GROUPS_PER_SLAB, SLAB_STATES).sum(axis=1)
    li = jnp.where(own, li, 0.0).reshape(slabs, GROUPS_PER_SLAB, SLAB_STATES).sum(axis=1)
    lam_ref[0] = lr
    lam_ref[1] = li
    lam2_ref[0] = lr * lr - li * li
    lam2_ref[1] = 2.0 * lr * li

    rep = lambda a: jnp.broadcast_to(a[:, None, :], (groups, chans, a.shape[-1])).reshape(
        groups * chans, a.shape[-1])
    _, _, coef_re, coef_im = _discretize(rep(a_re), rep(a_im), rep(dt))
    b_re = b_re_ref[...]
    b_im = b_im_ref[...]
    bbar_re = coef_re * b_re - coef_im * b_im
    bbar_im = coef_re * b_im + coef_im * b_re

    r2 = lax.broadcasted_iota(jnp.int32, (LANES, SLAB_STATES), 0)
    c2 = lax.broadcasted_iota(jnp.int32, (LANES, SLAB_STATES), 1)
    diag = (r2 // chans) == (c2 // n)
    block_diag = lambda m: jnp.where(diag, _tile_states(m), 0.0)
    zero = jnp.zeros((LANES, LANES), F32)
    contract_states = (((1,), (1,)), ((), ()))
    for j in range(slabs):
        rows = slice(LANES * j, LANES * (j + 1))
        lrj, lij = lr[j:j + 1, :], li[j:j + 1, :]
        b = jnp.concatenate([block_diag(bbar_re[rows]), block_diag(bbar_im[rows])], axis=-1)
        ct = jnp.concatenate([block_diag(c_re_ref[rows, :]), -block_diag(c_im_ref[rows, :])], axis=-1)
        lam_b = _scale_state_cols(b, lrj, lij)
        c_lam_t = _scale_state_cols_ct(ct, lrj, lij)
        c_lam2_t = _scale_state_cols_ct(c_lam_t, lrj, lij)
        bc_ref[j] = jnp.concatenate([lam_b, b], axis=0).astype(BF16)
        cc_ref[j] = jnp.concatenate([c_lam_t, c_lam2_t], axis=0).T.astype(BF16)
        cb = lax.dot_general(b, ct, contract_states, precision=lax.Precision.HIGHEST,
                             preferred_element_type=F32)
        c_lam_b = lax.dot_general(lam_b, ct, contract_states, precision=lax.Precision.HIGHEST,
                                  preferred_element_type=F32)
        d_ref[j] = jnp.concatenate([jnp.concatenate([cb, c_lam_b], axis=-1),
                                    jnp.concatenate([zero, cb], axis=-1)], axis=0).astype(BF16)


def _ssm_prep(a_re, a_im, log_dt, b_re, b_im, c_re, c_im):
    g, n = a_re.shape
    c = b_re.shape[-1]
    slabs = g // GROUPS_PER_SLAB
    ins = (a_re, a_im, log_dt.reshape(g, 1),
           b_re.transpose(0, 2, 1).reshape(g * c, n), b_im.transpose(0, 2, 1).reshape(g * c, n),
           c_re.reshape(g * c, n), c_im.reshape(g * c, n))
    outs = (jax.ShapeDtypeStruct((2, slabs, SLAB_STATES), F32),
            jax.ShapeDtypeStruct((2, slabs, SLAB_STATES), F32),
            jax.ShapeDtypeStruct((slabs, 2 * LANES, 2 * SLAB_STATES), BF16),
            jax.ShapeDtypeStruct((slabs, 2 * SLAB_STATES, 2 * LANES), BF16),
            jax.ShapeDtypeStruct((slabs, 2 * LANES, 2 * LANES), BF16))
    return pl.pallas_call(
        _ssm_prep_kernel,
        out_shape=outs,
        grid=(1,),
        in_specs=[_full_spec(x.shape) for x in ins],
        out_specs=[_full_spec(o.shape) for o in outs],
        name="ssm_prep",
    )(*ins)


def _pool_hist_kernel(cache_ref, hist_ref):
    for g, w in enumerate(POOL_WINDOWS):
        cols = slice(g * LANES, (g + 1) * LANES)
        acc = cache_ref[POOL_BUF - 1, :, cols]
        for s in range(2, w):
            acc = acc + cache_ref[POOL_BUF - s, :, cols]
        hist_ref[:, cols] = acc


def _pool_hist(cache_tm):
    _, n, width = cache_tm.shape
    return pl.pallas_call(
        _pool_hist_kernel,
        out_shape=jax.ShapeDtypeStruct((n, width), F32),
        grid=(1,),
        in_specs=[_full_spec(cache_tm.shape)],
        out_specs=_full_spec((n, width)),
        name="pool_hist",
    )(cache_tm)


def _pool_project(pooled, pool_w_ref, pool_scale_ref, pool_out_ref):
    mixed = [_dot(p.astype(BF16), pool_w_ref[g * LANES:(g + 1) * LANES, :])
             for g, p in enumerate(pooled)]
    yp = jnp.concatenate(mixed, axis=-1) * pool_scale_ref[...]
    return _dot(yp.astype(BF16), pool_out_ref[...])


def _finish_layer(x, h1, branch_pool, ys, w_in_ref, glu_ref, w_out_ref, g2_ref,
                  w_up_ref, w_down_ref):
    d = x.shape[-1]
    z = _dot(ys.astype(BF16), glu_ref[...])
    branch_ssm = z[:, :d] * _sigmoid(z[:, d:])
    gate_off = w_in_ref.shape[1] - 2 * d
    g_pool = _sigmoid(_dot(h1, w_in_ref[:, gate_off:gate_off + d]))
    g_ssm = _sigmoid(_dot(h1, w_in_ref[:, gate_off + d:]))
    merged = g_pool * branch_pool + g_ssm * branch_ssm
    x1 = x + _dot(merged.astype(BF16), w_out_ref[...])
    h2 = _rmsnorm(x1, g2_ref[...]).astype(BF16)
    d_ff = w_up_ref.shape[1]
    acc = None
    for c0 in range(0, d_ff, FF_CHUNK):
        a = jnp.maximum(_dot(h2, w_up_ref[:, c0:c0 + FF_CHUNK]), 0.0)
        part = _dot((a * a).astype(BF16), w_down_ref[c0:c0 + FF_CHUNK, :])
        acc = part if acc is None else acc + part
    return x1 + acc


def _load_weights_as_bf16(pairs, stage, sem):
    rows, stage_cols = stage.shape
    ring = stage_cols // W_CHUNK_COLS
    chunks = []
    for w_hbm, w_vm in pairs:
        k, n = w_hbm.shape
        cols = min(n, W_CHUNK_COLS)
        for r0 in range(0, k, rows):
            for c0 in range(0, n, cols):
                chunks.append((w_hbm, w_vm, r0, c0, cols))

    def copy(idx):
        w_hbm, _, r0, c0, cols = chunks[idx]
        sl = idx % ring
        return pltpu.make_async_copy(
            w_hbm.at[pl.ds(r0, rows), pl.ds(c0, cols)],
            stage.at[:, pl.ds(sl * W_CHUNK_COLS, cols)], sem.at[sl])

    for idx in range(min(ring, len(chunks))):
        copy(idx).start()
    for idx, (_, w_vm, r0, c0, cols) in enumerate(chunks):
        sl = idx % ring
        copy(idx).wait()
        w_vm[r0:r0 + rows, c0:c0 + cols] = stage[:, sl * W_CHUNK_COLS:sl * W_CHUNK_COLS + cols].astype(BF16)
        if idx + ring < len(chunks):
            copy(idx + ring).start()


def _layer_kernel(n_tiles,
                  x_hbm, w_in_hbm, pool_w_hbm, pool_out_hbm, glu_hbm, w_out_hbm, w_up_hbm, w_down_hbm,
                  g1_ref, pool_scale_ref, lam_ref, lam2_ref, bc_ref, cc_ref, d_ref, d_skip_ref,
                  g2_ref, gf_ref,
                  xs_ref, hist_ref, h0t_re_ref, h0t_im_ref, cache_hbm,
                  y_hbm, pool_s_hbm, pool_tail_ref, h_fin_ref, ys_out_ref, hst_re_ref, hst_im_ref,
                  xbuf, ybuf, sem_in, sem_out, sem_w, sem_pool, h1buf, x2buf, vbuf, s_buf, h_state, vs_buf,
                  w_in_ref, pool_w_ref, pool_out_ref, glu_ref, w_out_ref, w_up_ref, w_down_ref):
    i = pl.program_id(0)
    n_x, steps, batch, d_model = xbuf.shape
    rows = steps * batch
    halo = HALO_STEPS * batch
    n_slabs = bc_ref.shape[0]
    pool_width = vbuf.shape[1]

    def in_copy(tile, b):
        sl = lax.rem(tile, n_x)
        return pltpu.make_async_copy(x_hbm.at[b, pl.ds(tile * steps, steps), :],
                                     xbuf.at[sl, :, b, :], sem_in.at[sl, b])

    def out_copy(tile, b):
        sl = lax.rem(tile, 2)
        return pltpu.make_async_copy(ybuf.at[sl, :, b, :],
                                     y_hbm.at[b, pl.ds(tile * steps, steps), :], sem_out.at[sl, b])

    def final_norm_to_ybuf(sl):
        y = _rmsnorm(x2buf[sl], gf_ref[...])
        ybuf[sl] = y.reshape(steps, batch, d_model)

    def cache_shift_copy():
        return pltpu.make_async_copy(cache_hbm.at[pl.ds(1, POOL_BUF - 1)],
                                     pool_s_hbm.at[pl.ds(0, POOL_BUF - 1)], sem_pool.at[0])

    def cache_new_row_copy():
        return pltpu.make_async_copy(vs_buf, pool_s_hbm.at[POOL_BUF - 1], sem_pool.at[1])

    def ssm_cols(j):
        re_cols = slice(2 * SLAB_STATES * j, 2 * SLAB_STATES * j + SLAB_STATES)
        im_cols = slice(2 * SLAB_STATES * j + SLAB_STATES, 2 * SLAB_STATES * (j + 1))
        return slice(SLAB_STATES * j, SLAB_STATES * (j + 1)), re_cols, im_cols

    @pl.when(i == 0)
    def _():
        for b in range(batch):
            in_copy(0, b).start()
        for b in range(batch):
            in_copy(1, b).start()
        cache_shift_copy().start()
        _load_weights_as_bf16(
            [(w_in_hbm, w_in_ref), (pool_w_hbm, pool_w_ref), (pool_out_hbm, pool_out_ref),
             (glu_hbm, glu_ref), (w_out_hbm, w_out_ref), (w_up_hbm, w_up_ref),
             (w_down_hbm, w_down_ref)], s_buf, sem_w)
        vbuf[0:halo, :] = jnp.zeros((halo, pool_width), F32)
        h_state[...] = jnp.zeros(h_state.shape, F32)
        x2buf[1] = jnp.zeros(x2buf.shape[1:], F32)
        for b in range(batch):
            in_copy(0, b).wait()
        h1buf[0] = _rmsnorm(xbuf[0].reshape(rows, d_model), g1_ref[...]).astype(BF16)

    @pl.when(i < n_tiles)
    def _prompt_step():
        xslot = lax.rem(i, n_x)
        xslot_next = lax.rem(i + 1, n_x)
        par = lax.rem(i, 2)

        @pl.when(i + 2 < n_tiles)
        def _():
            for b in range(batch):
                in_copy(i + 2, b).start()

        @pl.when(i + 1 < n_tiles)
        def _():
            for b in range(batch):
                in_copy(i + 1, b).wait()

        @pl.when(i >= 3)
        def _():
            for b in range(batch):
                out_copy(i - 3, b).wait()

        final_norm_to_ybuf(1 - par)

        x = xbuf[xslot].reshape(rows, d_model)
        h1 = h1buf[par]
        v = _dot(h1, w_in_ref[:, 0:pool_width])
        vbuf[halo:halo + rows, :] = v
        u = _dot(h1, w_in_ref[:, pool_width:2 * pool_width])

        row = lax.broadcasted_iota(jnp.int32, (rows, LANES), 0)
        pos = i * steps + lax.shift_right_logical(row, int(math.log2(batch))) + 1
        pooled = []
        for g, w in enumerate(POOL_WINDOWS):
            cols = slice(g * LANES, (g + 1) * LANES)
            cur = vbuf[halo:halo + rows, cols]
            win = cur
            for s in range(1, w):
                win = win + vbuf[halo - s * batch:halo - s * batch + rows, cols]
            cnt = jnp.minimum(w, pos).astype(F32)
            pooled.append(win / cnt - cur)
        branch_pool = _pool_project(pooled, pool_w_ref, pool_scale_ref, pool_out_ref)

        pairs = steps // 2
        half = pairs * batch
        ssm_width = u.shape[1]
        u4 = u.reshape(pairs, 2, batch, ssm_width)
        u_even = u4[:, 0].reshape(half, ssm_width)
        u_odd = u4[:, 1].reshape(half, ssm_width)
        u_pair = [jnp.concatenate([u_even[:, LANES * j:LANES * (j + 1)],
                                   u_odd[:, LANES * j:LANES * (j + 1)]], axis=-1).astype(BF16)
                  for j in range(n_slabs)]
        for j in range(n_slabs):
            s_buf[0:half, 2 * SLAB_STATES * j:2 * SLAB_STATES * (j + 1)] = _dot(u_pair[j], bc_ref[j])
        for j in range(n_slabs):
            _, re_cols, im_cols = ssm_cols(j)
            lr2 = jnp.broadcast_to(lam2_ref[0, j:j + 1, :], (batch, SLAB_STATES))
            li2 = jnp.broadcast_to(lam2_ref[1, j:j + 1, :], (batch, SLAB_STATES))
            hr = h_state[:, re_cols]
            hi = h_state[:, im_cols]
            for p in range(pairs):
                r = slice(p * batch, (p + 1) * batch)
                sr = s_buf[r, re_cols]
                si = s_buf[r, im_cols]
                s_buf[r, re_cols] = hr
                s_buf[r, im_cols] = hi
                hr, hi = lr2 * hr - li2 * hi + sr, lr2 * hi + li2 * hr + si
            h_state[:, re_cols] = hr
            h_state[:, im_cols] = hi
        y_pair = [_dot(s_buf[0:half, 2 * SLAB_STATES * j:2 * SLAB_STATES * (j + 1)].astype(BF16), cc_ref[j])
                  + _dot(u_pair[j], d_ref[j]) for j in range(n_slabs)]
        y_even = jnp.concatenate([y[:, :LANES] for y in y_pair], axis=-1)
        y_odd = jnp.concatenate([y[:, LANES:] for y in y_pair], axis=-1)
        ys = jnp.concatenate([y_even.reshape(pairs, 1, batch, ssm_width),
                              y_odd.reshape(pairs, 1, batch, ssm_width)], axis=1).reshape(rows, ssm_width)
        ys = ys + d_skip_ref[...] * u

        x2buf[par] = _finish_layer(x, h1, branch_pool, ys, w_in_ref, glu_ref, w_out_ref, g2_ref,
                                   w_up_ref, w_down_ref)

        tail = vbuf[rows:rows + halo, :]
        vbuf[0:halo, :] = tail
        pool_tail_ref[...] = tail[halo - POOL_BUF * batch:, :]
        h_fin_ref[...] = h_state[...]

        h1buf[1 - par] = _rmsnorm(xbuf[xslot_next].reshape(rows, d_model), g1_ref[...]).astype(BF16)

        @pl.when(i >= 1)
        def _():
            for b in range(batch):
                out_copy(i - 1, b).start()

        @pl.when(i == n_tiles - 1)
        def _():
            for b in range(batch):
                out_copy(i - 2, b).wait()
            final_norm_to_ybuf(par)
            for b in range(batch):
                out_copy(i, b).start()
            for b in range(batch):
                out_copy(i - 1, b).wait()
            for b in range(batch):
                out_copy(i, b).wait()

    @pl.when(i == n_tiles)
    def _decode_step():
        x = xs_ref[...]
        h1 = _rmsnorm(x, g1_ref[...]).astype(BF16)
        v = _dot(h1, w_in_ref[:, 0:pool_width])
        vs_buf[...] = v
        cache_new_row_copy().start()
        u = _dot(h1, w_in_ref[:, pool_width:2 * pool_width])

        pooled = []
        for g, w in enumerate(POOL_WINDOWS):
            cols = slice(g * LANES, (g + 1) * LANES)
            cur = v[:, cols]
            cnt = float(min(w, PAST_LEN + 1))
            pooled.append((hist_ref[:, cols] + cur) / cnt - cur)
        branch_pool = _pool_project(pooled, pool_w_ref, pool_scale_ref, pool_out_ref)

        u_bf = u.astype(BF16)
        ys_parts = []
        for j in range(n_slabs):
            st, _, _ = ssm_cols(j)
            u_j = u_bf[:, LANES * j:LANES * (j + 1)]
            bu = _dot(u_j, bc_ref[j, LANES:2 * LANES, :])
            lr = lam_ref[0, j:j + 1, :]
            li = lam_ref[1, j:j + 1, :]
            hr = h0t_re_ref[st, :].T
            hi = h0t_im_ref[st, :].T
            nr = lr * hr - li * hi + bu[:, :SLAB_STATES]
            ni = lr * hi + li * hr + bu[:, SLAB_STATES:]
            hst_re_ref[st, :] = nr.T
            hst_im_ref[st, :] = ni.T
            h_prev = jnp.concatenate([hr, hi], axis=-1).astype(BF16)
            ys_parts.append(_dot(h_prev, cc_ref[j, :, 0:LANES]) + _dot(u_j, d_ref[j, 0:LANES, 0:LANES]))
        ys = jnp.concatenate(ys_parts, axis=-1) + d_skip_ref[...] * u

        x2 = _finish_layer(x, h1, branch_pool, ys, w_in_ref, glu_ref, w_out_ref, g2_ref,
                           w_up_ref, w_down_ref)
        ys_out_ref[...] = _rmsnorm(x2, gf_ref[...])
        cache_shift_copy().wait()
        cache_new_row_copy().wait()


def _const_spec(shape):
    return pl.BlockSpec(shape, lambda i: (0,) * len(shape), pipeline_mode=pl.Buffered(1))


def _layer_call(x, big_weights, small_params, sample_ins):
    batch, seq, d = x.shape
    steps = STEPS_PER_TILE
    rows = steps * batch
    n_tiles = seq // steps
    assert n_tiles >= X_SLOTS, "the input ring and the deferred final norm assume at least 3 tiles"
    xs, hist, h0t_re, _, cache_tm = sample_ins
    dec_rows = xs.shape[0]
    pool_width = hist.shape[-1]
    n_states = h0t_re.shape[0]
    halo = HALO_STEPS * batch
    assert all(w.shape[0] % rows == 0 and w.shape[1] % min(w.shape[1], W_CHUNK_COLS) == 0
               for w in big_weights), "weight staging chunks must tile every weight"
    out_shape = (jax.ShapeDtypeStruct((batch, seq, d), F32),
                 jax.ShapeDtypeStruct(cache_tm.shape, F32),
                 jax.ShapeDtypeStruct((POOL_BUF * batch, pool_width), F32),
                 jax.ShapeDtypeStruct((batch, 2 * n_states), F32),
                 jax.ShapeDtypeStruct((dec_rows, d), F32),
                 jax.ShapeDtypeStruct((n_states, dec_rows), F32),
                 jax.ShapeDtypeStruct((n_states, dec_rows), F32))
    any_spec = pl.BlockSpec(memory_space=pl.ANY)
    return pl.pallas_call(
        functools.partial(_layer_kernel, n_tiles),
        out_shape=out_shape,
        grid=(n_tiles + 1,),
        in_specs=([any_spec] * (1 + len(big_weights))
                  + [_const_spec(p.shape) for p in small_params]
                  + [_const_spec(a.shape) for a in sample_ins[:-1]] + [any_spec]),
        out_specs=[any_spec, any_spec] + [_full_spec(o.shape) for o in out_shape[2:]],
        scratch_shapes=[pltpu.VMEM((X_SLOTS, steps, batch, d), F32),
                        pltpu.VMEM((2, steps, batch, d), F32),
                        pltpu.SemaphoreType.DMA((X_SLOTS, batch)),
                        pltpu.SemaphoreType.DMA((2, batch)),
                        pltpu.SemaphoreType.DMA((2 * n_states // W_CHUNK_COLS,)),
                        pltpu.SemaphoreType.DMA((2,)),
                        pltpu.VMEM((2, rows, d), BF16),
                        pltpu.VMEM((2, rows, d), F32),
                        pltpu.VMEM((halo + rows, pool_width), F32),
                        pltpu.VMEM((rows, 2 * n_states), F32),
                        pltpu.VMEM((batch, 2 * n_states), F32),
                        pltpu.VMEM((dec_rows, pool_width), F32)]
                       + [pltpu.VMEM(w.shape, BF16) for w in big_weights],
        compiler_params=pltpu.CompilerParams(dimension_semantics=("arbitrary",),
                                             vmem_limit_bytes=VMEM_LIMIT_BYTES),
        name="layer",
    )(x, *big_weights, *small_params, *sample_ins)


def kernel(x_prompt, x_sample, cache_pool, state_ssm_re, state_ssm_im, norm1_g, w_in, pool_w,
           pool_scale, pool_out, ssm_a_re, ssm_a_im, ssm_log_dt, ssm_b_re, ssm_b_im, ssm_c_re,
           ssm_c_im, ssm_d, ssm_glu, w_out, norm2_g, w_up, w_down, normf_g):
    depth = w_in.shape[0]
    assert depth == 1, "the final norm is fused into the (single) layer kernel"
    batch, seq, d = x_prompt.shape
    dec_batch, dec_seq, _ = x_sample.shape
    assert dec_seq == 1 and batch == SUBLANES and seq % STEPS_PER_TILE == 0
    groups, n_state = ssm_a_re.shape[1:]
    n_states = groups * n_state
    l = 0

    ssm_operands = _ssm_prep(ssm_a_re[l], ssm_a_im[l], ssm_log_dt[l], ssm_b_re[l], ssm_b_im[l],
                             ssm_c_re[l], ssm_c_im[l])
    row = lambda a: a.reshape(1, -1)
    pool_groups, pool_gw, _ = pool_w[l].shape
    big_weights = (w_in[l], pool_w[l].reshape(pool_groups * pool_gw, pool_gw), pool_out[l],
                   ssm_glu[l], w_out[l], w_up[l], w_down[l])
    small_params = (row(norm1_g[l]), row(pool_scale[l]), *ssm_operands,
                    row(ssm_d[l]), row(norm2_g[l]), row(normf_g))
    cache_tm = cache_pool[l].transpose(1, 0, 2)
    state_t = lambda s: s.transpose(1, 2, 0).reshape(n_states, dec_batch)
    sample_ins = (x_sample.reshape(dec_batch, d), _pool_hist(cache_tm),
                  state_t(state_ssm_re[l]), state_t(state_ssm_im[l]), cache_tm)

    y_prompt, pool_s_tm, pool_tail, h_fin, y_s, re_s_t, im_s_t = _layer_call(
        x_prompt, big_weights, small_params, sample_ins)

    pool_width = pool_tail.shape[-1]
    pool_p = pool_tail.reshape(POOL_BUF, batch, pool_width).transpose(1, 0, 2)
    h_fin = h_fin.reshape(batch, n_states // SLAB_STATES, 2, SLAB_STATES)
    re_p = h_fin[:, :, 0].reshape(batch, groups, n_state)
    im_p = h_fin[:, :, 1].reshape(batch, groups, n_state)
    pool_s = pool_s_tm.transpose(1, 0, 2)
    state_back = lambda s: s.reshape(groups, n_state, dec_batch).transpose(2, 0, 1)

    return (y_prompt, y_s.reshape(dec_batch, dec_seq, d),
            pool_p[None], re_p[None], im_p[None],
            pool_s[None], state_back(re_s_t)[None], state_back(im_s_t)[None])
```

```python
import functools
import math

import jax
import jax.numpy as jnp
from jax import lax
from jax.experimental import pallas as pl
from jax.experimental.pallas import tpu as pltpu

POOL_WINDOWS = (2, 4, 8, 16)
POOL_BUF = max(POOL_WINDOWS) - 1
SSM_GROUP_W = 16
SSM_STATE = 64
RMS_EPS = 1e-6
PAST_LEN = 16384

LANES = 128
SUBLANES = 8
GROUPS_PER_SLAB = LANES // SSM_GROUP_W
SLAB_STATES = GROUPS_PER_SLAB * SSM_STATE
STEPS_PER_TILE = 32
HALO_STEPS = 16
X_SLOTS = 3
FF_CHUNK = 1024
W_CHUNK_COLS = 1024
VMEM_LIMIT_BYTES = 56 * 1024 * 1024

F32 = jnp.float32
BF16 = jnp.bfloat16


def _dot(a, b):
    return jnp.dot(a, b, preferred_element_type=F32)


def _sigmoid(x):
    return 1.0 / (1.0 + jnp.exp(-x))


def _rmsnorm(x, g):
    y = x * lax.rsqrt(jnp.mean(x * x, axis=-1, keepdims=True) + RMS_EPS)
    return y * g


def _full_spec(shape):
    return pl.BlockSpec(shape, lambda i: (0,) * len(shape))


def _discretize(a_re, a_im, dt):
    mag = jnp.exp(a_re * dt)
    ang = a_im * dt
    lam_re = mag * jnp.cos(ang)
    lam_im = mag * jnp.sin(ang)
    p = lam_re - 1.0
    q = lam_im
    den = a_re * a_re + a_im * a_im
    coef_re = (p * a_re + q * a_im) / den
    coef_im = (q * a_re - p * a_im) / den
    return lam_re, lam_im, coef_re, coef_im


def _tile_states(x):
    x2 = jnp.concatenate([x, x], axis=-1)
    return jnp.concatenate([x2] * (SLAB_STATES // (2 * SSM_STATE)), axis=-1)


def _scale_state_cols(m, lr, li):
    re, im = m[:, :SLAB_STATES], m[:, SLAB_STATES:]
    return jnp.concatenate([lr * re - li * im, lr * im + li * re], axis=-1)


def _scale_state_cols_ct(ct, lr, li):
    re, m = ct[:, :SLAB_STATES], ct[:, SLAB_STATES:]
    return jnp.concatenate([lr * re + li * m, lr * m - li * re], axis=-1)


def _ssm_prep_kernel(a_re_ref, a_im_ref, log_dt_ref, b_re_ref, b_im_ref, c_re_ref, c_im_ref,
                     lam_ref, lam2_ref, bc_ref, cc_ref, d_ref):
    groups, n = a_re_ref.shape
    slabs = bc_ref.shape[0]
    chans = b_re_ref.shape[0] // groups
    dt = jnp.exp(log_dt_ref[...])
    a_re = a_re_ref[...]
    a_im = a_im_ref[...]

    lr, li, _, _ = _discretize(_tile_states(a_re), _tile_states(a_im), dt)
    row = lax.broadcasted_iota(jnp.int32, (groups, SLAB_STATES), 0)
    col = lax.broadcasted_iota(jnp.int32, (groups, SLAB_STATES), 1)
    own = (row % GROUPS_PER_SLAB) == (col // n)
    lr = jnp.where(own, lr, 0.0).reshape(slabs, GROUPS_PER_SLAB, SLAB_STATES).sum(axis=1)
    li = jnp.where(own, li, 0.0).reshape(slabs, GROUPS_PER_SLAB, SLAB_STATES).sum(axis=1)
    lam_ref[0] = lr
    lam_ref[1] = li
    lam2_ref[0] = lr * lr - li * li
    lam2_ref[1] = 2.0 * lr * li

    rep = lambda a: jnp.broadcast_to(a[:, None, :], (groups, chans, a.shape[-1])).reshape(
        groups * chans, a.shape[-1])
    _, _, coef_re, coef_im = _discretize(rep(a_re), rep(a_im), rep(dt))
    b_re = b_re_ref[...]
    b_im = b_im_ref[...]
    bbar_re = coef_re * b_re - coef_im * b_im
    bbar_im = coef_re * b_im + coef_im * b_re

    r2 = lax.broadcasted_iota(jnp.int32, (LANES, SLAB_STATES), 0)
    c2 = lax.broadcasted_iota(jnp.int32, (LANES, SLAB_STATES), 1)
    diag = (r2 // chans) == (c2 // n)
    block_diag = lambda m: jnp.where(diag, _tile_states(m), 0.0)
    zero = jnp.zeros((LANES, LANES), F32)
    contract_states = (((1,), (1,)), ((), ()))
    for j in range(slabs):
        rows = slice(LANES * j, LANES * (j + 1))
        lrj, lij = lr[j:j + 1, :], li[j:j + 1, :]
        b = jnp.concatenate([block_diag(bbar_re[rows]), block_diag(bbar_im[rows])], axis=-1)
        ct = jnp.concatenate([block_diag(c_re_ref[rows, :]), -block_diag(c_im_ref[rows, :])], axis=-1)
        lam_b = _scale_state_cols(b, lrj, lij)
        c_lam_t = _scale_state_cols_ct(ct, lrj, lij)
        c_lam2_t = _scale_state_cols_ct(c_lam_t, lrj, lij)
        bc_ref[j] = jnp.concatenate([lam_b, b], axis=0).astype(BF16)
        cc_ref[j] = jnp.concatenate([c_lam_t, c_lam2_t], axis=0).T.astype(BF16)
        cb = lax.dot_general(b, ct, contract_states, precision=lax.Precision.HIGHEST,
                             preferred_element_type=F32)
        c_lam_b = lax.dot_general(lam_b, ct, contract_states, precision=lax.Precision.HIGHEST,
                                  preferred_element_type=F32)
        d_ref[j] = jnp.concatenate([jnp.concatenate([cb, c_lam_b], axis=-1),
                                    jnp.concatenate([zero, cb], axis=-1)], axis=0).astype(BF16)


def _ssm_prep(a_re, a_im, log_dt, b_re, b_im, c_re, c_im):
    g, n = a_re.shape
    c = b_re.shape[-1]
    slabs = g // GROUPS_PER_SLAB
    ins = (a_re, a_im, log_dt.reshape(g, 1),
           b_re.transpose(0, 2, 1).reshape(g * c, n), b_im.transpose(0, 2, 1).reshape(g * c, n),
           c_re.reshape(g * c, n), c_im.reshape(g * c, n))
    outs = (jax.ShapeDtypeStruct((2, slabs, SLAB_STATES), F32),
            jax.ShapeDtypeStruct((2, slabs, SLAB_STATES), F32),
            jax.ShapeDtypeStruct((slabs, 2 * LANES, 2 * SLAB_STATES), BF16),
            jax.ShapeDtypeStruct((slabs, 2 * SLAB_STATES, 2 * LANES), BF16),
            jax.ShapeDtypeStruct((slabs, 2 * LANES, 2 * LANES), BF16))
    return pl.pallas_call(
        _ssm_prep_kernel,
        out_shape=outs,
        grid=(1,),
        in_specs=[_full_spec(x.shape) for x in ins],
        out_specs=[_full_spec(o.shape) for o in outs],
        name="ssm_prep",
    )(*ins)


def _pool_hist_kernel(cache_ref, hist_ref):
    for g, w in enumerate(POOL_WINDOWS):
        cols = slice(g * LANES, (g + 1) * LANES)
        acc = cache_ref[POOL_BUF - 1, :, cols]
        for s in range(2, w):
            acc = acc + cache_ref[POOL_BUF - s, :, cols]
        hist_ref[:, cols] = acc


def _pool_hist(cache_tm):
    _, n, width = cache_tm.shape
    return pl.pallas_call(
        _pool_hist_kernel,
        out_shape=jax.ShapeDtypeStruct((n, width), F32),
        grid=(1,),
        in_specs=[_full_spec(cache_tm.shape)],
        out_specs=_full_spec((n, width)),
        name="pool_hist",
    )(cache_tm)


def _pool_project(pooled, pool_w_ref, pool_scale_ref, pool_out_ref):
    mixed = [_dot(p.astype(BF16), pool_w_ref[g * LANES:(g + 1) * LANES, :])
             for g, p in enumerate(pooled)]
    yp = jnp.concatenate(mixed, axis=-1) * pool_scale_ref[...]
    return _dot(yp.astype(BF16), pool_out_ref[...])


def _finish_layer(x, h1, branch_pool, ys, w_in_ref, glu_ref, w_out_ref, g2_ref,
                  w_up_ref, w_down_ref):
    d = x.shape[-1]
    z = _dot(ys.astype(BF16), glu_ref[...])
    branch_ssm = z[:, :d] * _sigmoid(z[:, d:])
    gate_off = w_in_ref.shape[1] - 2 * d
    g_pool = _sigmoid(_dot(h1, w_in_ref[:, gate_off:gate_off + d]))
    g_ssm = _sigmoid(_dot(h1, w_in_ref[:, gate_off + d:]))
    merged = g_pool * branch_pool + g_ssm * branch_ssm
    x1 = x + _dot(merged.astype(BF16), w_out_ref[...])
    h2 = _rmsnorm(x1, g2_ref[...]).astype(BF16)
    d_ff = w_up_ref.shape[1]
    acc = None
    for c0 in range(0, d_ff, FF_CHUNK):
        a = jnp.maximum(_dot(h2, w_up_ref[:, c0:c0 + FF_CHUNK]), 0.0)
        part = _dot((a * a).astype(BF16), w_down_ref[c0:c0 + FF_CHUNK, :])
        acc = part if acc is None else acc + part
    return x1 + acc


def _load_weights_as_bf16(pairs, stage, sem):
    rows, stage_cols = stage.shape
    ring = stage_cols // W_CHUNK_COLS
    chunks = []
    for w_hbm, w_vm in pairs:
        k, n = w_hbm.shape
        cols = min(n, W_CHUNK_COLS)
        for r0 in range(0, k, rows):
            for c0 in range(0, n, cols):
                chunks.append((w_hbm, w_vm, r0, c0, cols))

    def copy(idx):
        w_hbm, _, r0, c0, cols = chunks[idx]
        sl = idx % ring
        return pltpu.make_async_copy(
            w_hbm.at[pl.ds(r0, rows), pl.ds(c0, cols)],
            stage.at[:, pl.ds(sl * W_CHUNK_COLS, cols)], sem.at[sl])

    for idx in range(min(ring, len(chunks))):
        copy(idx).start()
    for idx, (_, w_vm, r0, c0, cols) in enumerate(chunks):
        sl = idx % ring
        copy(idx).wait()
        w_vm[r0:r0 + rows, c0:c0 + cols] = stage[:, sl * W_CHUNK_COLS:sl * W_CHUNK_COLS + cols].astype(BF16)
        if idx + ring < len(chunks):
            copy(idx + ring).start()


def _layer_kernel(n_tiles,
                  x_hbm, w_in_hbm, pool_w_hbm, pool_out_hbm, glu_hbm, w_out_hbm, w_up_hbm, w_down_hbm,
                  g1_ref, pool_scale_ref, lam_ref, lam2_ref, bc_ref, cc_ref, d_ref, d_skip_ref,
                  g2_ref, gf_ref,
                  xs_ref, hist_ref, h0t_re_ref, h0t_im_ref, cache_hbm,
                  y_hbm, pool_s_hbm, pool_tail_ref, h_fin_ref, ys_out_ref, hst_re_ref, hst_im_ref,
                  xbuf, ybuf, sem_in, sem_out, sem_w, sem_pool, h1buf, x2buf, vbuf, s_buf, h_state, vs_buf,
                  w_in_ref, pool_w_ref, pool_out_ref, glu_ref, w_out_ref, w_up_ref, w_down_ref):
    i = pl.program_id(0)
    n_x, steps, batch, d_model = xbuf.shape
    rows = steps * batch
    halo = HALO_STEPS * batch
    n_slabs = bc_ref.shape[0]
    pool_width = vbuf.shape[1]

    def in_copy(tile, b):
        sl = lax.rem(tile, n_x)
        return pltpu.make_async_copy(x_hbm.at[b, pl.ds(tile * steps, steps), :],
                                     xbuf.at[sl, :, b, :], sem_in.at[sl, b])

    def out_copy(tile, b):
        sl = lax.rem(tile, 2)
        return pltpu.make_async_copy(ybuf.at[sl, :, b, :],
                                     y_hbm.at[b, pl.ds(tile * steps, steps), :], sem_out.at[sl, b])

    def final_norm_to_ybuf(sl):
        y = _rmsnorm(x2buf[sl], gf_ref[...])
        ybuf[sl] = y.reshape(steps, batch, d_model)

    def cache_shift_copy():
        return pltpu.make_async_copy(cache_hbm.at[pl.ds(1, POOL_BUF - 1)],
                                     pool_s_hbm.at[pl.ds(0, POOL_BUF - 1)], sem_pool.at[0])

    def cache_new_row_copy():
        return pltpu.make_async_copy(vs_buf, pool_s_hbm.at[POOL_BUF - 1], sem_pool.at[1])

    def ssm_cols(j):
        re_cols = slice(2 * SLAB_STATES * j, 2 * SLAB_STATES * j + SLAB_STATES)
        im_cols = slice(2 * SLAB_STATES * j + SLAB_STATES, 2 * SLAB_STATES * (j + 1))
        return slice(SLAB_STATES * j, SLAB_STATES * (j + 1)), re_cols, im_cols

    @pl.when(i == 0)
    def _():
        for b in range(batch):
            in_copy(0, b).start()
        for b in range(batch):
            in_copy(1, b).start()
        cache_shift_copy().start()
        _load_weights_as_bf16(
            [(w_in_hbm, w_in_ref), (pool_w_hbm, pool_w_ref), (pool_out_hbm, pool_out_ref),
             (glu_hbm, glu_ref), (w_out_hbm, w_out_ref), (w_up_hbm, w_up_ref),
             (w_down_hbm, w_down_ref)], s_buf, sem_w)
        vbuf[0:halo, :] = jnp.zeros((halo, pool_width), F32)
        h_state[...] = jnp.zeros(h_state.shape, F32)
        x2buf[1] = jnp.zeros(x2buf.shape[1:], F32)
        for b in range(batch):
            in_copy(0, b).wait()
        h1buf[0] = _rmsnorm(xbuf[0].reshape(rows, d_model), g1_ref[...]).astype(BF16)

    @pl.when(i < n_tiles)
    def _prompt_step():
        xslot = lax.rem(i, n_x)
        xslot_next = lax.rem(i + 1, n_x)
        par = lax.rem(i, 2)

        @pl.when(i + 2 < n_tiles)
        def _():
            for b in range(batch):
                in_copy(i + 2, b).start()

        @pl.when(i + 1 < n_tiles)
        def _():
            for b in range(batch):
                in_copy(i + 1, b).wait()

        @pl.when(i >= 3)
        def _():
            for b in range(batch):
                out_copy(i - 3, b).wait()

        def tile_block(par):
            final_norm_to_ybuf(1 - par)
            h1buf[1 - par] = _rmsnorm(xbuf[xslot_next].reshape(rows, d_model), g1_ref[...]).astype(BF16)

            x = xbuf[xslot].reshape(rows, d_model)
            h1 = h1buf[par]
            v = _dot(h1, w_in_ref[:, 0:pool_width])
            vbuf[halo:halo + rows, :] = v
            u = _dot(h1, w_in_ref[:, pool_width:2 * pool_width])

            row = lax.broadcasted_iota(jnp.int32, (rows, LANES), 0)
            pos = i * steps + lax.shift_right_logical(row, int(math.log2(batch))) + 1
            pooled = []
            for g, w in enumerate(POOL_WINDOWS):
                cols = slice(g * LANES, (g + 1) * LANES)
                cur = vbuf[halo:halo + rows, cols]
                win = cur
                for s in range(1, w):
                    win = win + vbuf[halo - s * batch:halo - s * batch + rows, cols]
                cnt = jnp.minimum(w, pos).astype(F32)
                pooled.append(win / cnt - cur)
            branch_pool = _pool_project(pooled, pool_w_ref, pool_scale_ref, pool_out_ref)

            pairs = steps // 2
            half = pairs * batch
            ssm_width = u.shape[1]
            u4 = u.reshape(pairs, 2, batch, ssm_width)
            u_even = u4[:, 0].reshape(half, ssm_width)
            u_odd = u4[:, 1].reshape(half, ssm_width)
            u_pair = [jnp.concatenate([u_even[:, LANES * j:LANES * (j + 1)],
                                       u_odd[:, LANES * j:LANES * (j + 1)]], axis=-1).astype(BF16)
                      for j in range(n_slabs)]
            for j in range(n_slabs):
                s_buf[0:half, 2 * SLAB_STATES * j:2 * SLAB_STATES * (j + 1)] = _dot(u_pair[j], bc_ref[j])
            for j in range(n_slabs):
                _, re_cols, im_cols = ssm_cols(j)
                lr2 = jnp.broadcast_to(lam2_ref[0, j:j + 1, :], (batch, SLAB_STATES))
                li2 = jnp.broadcast_to(lam2_ref[1, j:j + 1, :], (batch, SLAB_STATES))
                hr = h_state[:, re_cols]
                hi = h_state[:, im_cols]
                for p in range(pairs):
                    r = slice(p * batch, (p + 1) * batch)
                    sr = s_buf[r, re_cols]
                    si = s_buf[r, im_cols]
                    s_buf[r, re_cols] = hr
                    s_buf[r, im_cols] = hi
                    hr, hi = lr2 * hr - li2 * hi + sr, lr2 * hi + li2 * hr + si
                h_state[:, re_cols] = hr
                h_state[:, im_cols] = hi
            y_pair = [_dot(s_buf[0:half, 2 * SLAB_STATES * j:2 * SLAB_STATES * (j + 1)].astype(BF16),
                           cc_ref[j]) + _dot(u_pair[j], d_ref[j]) for j in range(n_slabs)]
            y_even = jnp.concatenate([y[:, :LANES] for y in y_pair], axis=-1)
            y_odd = jnp.concatenate([y[:, LANES:] for y in y_pair], axis=-1)
            ys = jnp.concatenate([y_even.reshape(pairs, 1, batch, ssm_width),
                                  y_odd.reshape(pairs, 1, batch, ssm_width)], axis=1).reshape(rows, ssm_width)
            ys = ys + d_skip_ref[...] * u

            x2buf[par] = _finish_layer(x, h1, branch_pool, ys, w_in_ref, glu_ref, w_out_ref, g2_ref,
                                       w_up_ref, w_down_ref)

            tail = vbuf[rows:rows + halo, :]
            vbuf[0:halo, :] = tail
            pool_tail_ref[...] = tail[halo - POOL_BUF * batch:, :]
            h_fin_ref[...] = h_state[...]

        for parity in (0, 1):
            pl.when(par == parity)(functools.partial(tile_block, parity))

        @pl.when(i >= 1)
        def _():
            for b in range(batch):
                out_copy(i - 1, b).start()

        @pl.when(i == n_tiles - 1)
        def _():
            for b in range(batch):
                out_copy(i - 2, b).wait()
            final_norm_to_ybuf(par)
            for b in range(batch):
                out_copy(i, b).start()
            for b in range(batch):
                out_copy(i - 1, b).wait()
            for b in range(batch):
                out_copy(i, b).wait()

    @pl.when(i == n_tiles)
    def _decode_step():
        x = xs_ref[...]
        h1 = _rmsnorm(x, g1_ref[...]).astype(BF16)
        v = _dot(h1, w_in_ref[:, 0:pool_width])
        vs_buf[...] = v
        cache_new_row_copy().start()
        u = _dot(h1, w_in_ref[:, pool_width:2 * pool_width])

        pooled = []
        for g, w in enumerate(POOL_WINDOWS):
            cols = slice(g * LANES, (g + 1) * LANES)
            cur = v[:, cols]
            cnt = float(min(w, PAST_LEN + 1))
            pooled.append((hist_ref[:, cols] + cur) / cnt - cur)
        branch_pool = _pool_project(pooled, pool_w_ref, pool_scale_ref, pool_out_ref)

        u_bf = u.astype(BF16)
        ys_parts = []
        for j in range(n_slabs):
            st, _, _ = ssm_cols(j)
            u_j = u_bf[:, LANES * j:LANES * (j + 1)]
            bu = _dot(u_j, bc_ref[j, LANES:2 * LANES, :])
            lr = lam_ref[0, j:j + 1, :]
            li = lam_ref[1, j:j + 1, :]
            hr = h0t_re_ref[st, :].T
            hi = h0t_im_ref[st, :].T
            nr = lr * hr - li * hi + bu[:, :SLAB_STATES]
            ni = lr * hi + li * hr + bu[:, SLAB_STATES:]
            hst_re_ref[st, :] = nr.T
            hst_im_ref[st, :] = ni.T
            h_prev = jnp.concatenate([hr, hi], axis=-1).astype(BF16)
            ys_parts.append(_dot(h_prev, cc_ref[j, :, 0:LANES]) + _dot(u_j, d_ref[j, 0:LANES, 0:LANES]))
        ys = jnp.concatenate(ys_parts, axis=-1) + d_skip_ref[...] * u

        x2 = _finish_layer(x, h1, branch_pool, ys, w_in_ref, glu_ref, w_out_ref, g2_ref,
                           w_up_ref, w_down_ref)
        ys_out_ref[...] = _rmsnorm(x2, gf_ref[...])
        cache_shift_copy().wait()
        cache_new_row_copy().wait()


def _const_spec(shape):
    return pl.BlockSpec(shape, lambda i: (0,) * len(shape), pipeline_mode=pl.Buffered(1))


def _layer_call(x, big_weights, small_params, sample_ins):
    batch, seq, d = x.shape
    steps = STEPS_PER_TILE
    rows = steps * batch
    n_tiles = seq // steps
    assert n_tiles >= X_SLOTS, "the input ring and the deferred final norm assume at least 3 tiles"
    xs, hist, h0t_re, _, cache_tm = sample_ins
    dec_rows = xs.shape[0]
    pool_width = hist.shape[-1]
    n_states = h0t_re.shape[0]
    halo = HALO_STEPS * batch
    assert all(w.shape[0] % rows == 0 and w.shape[1] % min(w.shape[1], W_CHUNK_COLS) == 0
               for w in big_weights), "weight staging chunks must tile every weight"
    out_shape = (jax.ShapeDtypeStruct((batch, seq, d), F32),
                 jax.ShapeDtypeStruct(cache_tm.shape, F32),
                 jax.ShapeDtypeStruct((POOL_BUF * batch, pool_width), F32),
                 jax.ShapeDtypeStruct((batch, 2 * n_states), F32),
                 jax.ShapeDtypeStruct((dec_rows, d), F32),
                 jax.ShapeDtypeStruct((n_states, dec_rows), F32),
                 jax.ShapeDtypeStruct((n_states, dec_rows), F32))
    any_spec = pl.BlockSpec(memory_space=pl.ANY)
    return pl.pallas_call(
        functools.partial(_layer_kernel, n_tiles),
        out_shape=out_shape,
        grid=(n_tiles + 1,),
        in_specs=([any_spec] * (1 + len(big_weights))
                  + [_const_spec(p.shape) for p in small_params]
                  + [_const_spec(a.shape) for a in sample_ins[:-1]] + [any_spec]),
        out_specs=[any_spec, any_spec] + [_full_spec(o.shape) for o in out_shape[2:]],
        scratch_shapes=[pltpu.VMEM((X_SLOTS, steps, batch, d), F32),
                        pltpu.VMEM((2, steps, batch, d), F32),
                        pltpu.SemaphoreType.DMA((X_SLOTS, batch)),
                        pltpu.SemaphoreType.DMA((2, batch)),
                        pltpu.SemaphoreType.DMA((2 * n_states // W_CHUNK_COLS,)),
                        pltpu.SemaphoreType.DMA((2,)),
                        pltpu.VMEM((2, rows, d), BF16),
                        pltpu.VMEM((2, rows, d), F32),
                        pltpu.VMEM((halo + rows, pool_width), F32),
                        pltpu.VMEM((rows, 2 * n_states), F32),
                        pltpu.VMEM((batch, 2 * n_states), F32),
                        pltpu.VMEM((dec_rows, pool_width), F32)]
                       + [pltpu.VMEM(w.shape, BF16) for w in big_weights],
        compiler_params=pltpu.CompilerParams(dimension_semantics=("arbitrary",),
                                             vmem_limit_bytes=VMEM_LIMIT_BYTES),
        name="layer",
    )(x, *big_weights, *small_params, *sample_ins)


def kernel(x_prompt, x_sample, cache_pool, state_ssm_re, state_ssm_im, norm1_g, w_in, pool_w,
           pool_scale, pool_out, ssm_a_re, ssm_a_im, ssm_log_dt, ssm_b_re, ssm_b_im, ssm_c_re,
           ssm_c_im, ssm_d, ssm_glu, w_out, norm2_g, w_up, w_down, normf_g):
    depth = w_in.shape[0]
    assert depth == 1, "the final norm is fused into the (single) layer kernel"
    batch, seq, d = x_prompt.shape
    dec_batch, dec_seq, _ = x_sample.shape
    assert dec_seq == 1 and batch == SUBLANES and seq % STEPS_PER_TILE == 0
    groups, n_state = ssm_a_re.shape[1:]
    n_states = groups * n_state
    l = 0

    ssm_operands = _ssm_prep(ssm_a_re[l], ssm_a_im[l], ssm_log_dt[l], ssm_b_re[l], ssm_b_im[l],
                             ssm_c_re[l], ssm_c_im[l])
    row = lambda a: a.reshape(1, -1)
    pool_groups, pool_gw, _ = pool_w[l].shape
    big_weights = (w_in[l], pool_w[l].reshape(pool_groups * pool_gw, pool_gw), pool_out[l],
                   ssm_glu[l], w_out[l], w_up[l], w_down[l])
    small_params = (row(norm1_g[l]), row(pool_scale[l]), *ssm_operands,
                    row(ssm_d[l]), row(norm2_g[l]), row(normf_g))
    cache_tm = cache_pool[l].transpose(1, 0, 2)
    state_t = lambda s: s.transpose(1, 2, 0).reshape(n_states, dec_batch)
    sample_ins = (x_sample.reshape(dec_batch, d), _pool_hist(cache_tm),
                  state_t(state_ssm_re[l]), state_t(state_ssm_im[l]), cache_tm)

    y_prompt, pool_s_tm, pool_tail, h_fin, y_s, re_s_t, im_s_t = _layer_call(
        x_prompt, big_weights, small_params, sample_ins)

    pool_width = pool_tail.shape[-1]
    pool_p = pool_tail.reshape(POOL_BUF, batch, pool_width).transpose(1, 0, 2)
    h_fin = h_fin.reshape(batch, n_states // SLAB_STATES, 2, SLAB_STATES)
    re_p = h_fin[:, :, 0].reshape(batch, groups, n_state)
    im_p = h_fin[:, :, 1].reshape(batch, groups, n_state)
    pool_s = pool_s_tm.transpose(1, 0, 2)
    state_back = lambda s: s.reshape(groups, n_state, dec_batch).transpose(2, 0, 1)

    return (y_prompt, y_s.reshape(dec_batch, dec_seq, d),
            pool_p[None], re_p[None], im_p[None],
            pool_s[None], state_back(re_s_t)[None], state_back(im_s_t)[None])
```

```python
import functools
import math

import jax
import jax.numpy as jnp
from jax import lax
from jax.experimental import pallas as pl
from jax.experimental.pallas import tpu as pltpu

POOL_WINDOWS = (2, 4, 8, 16)
POOL_BUF = max(POOL_WINDOWS) - 1
SSM_GROUP_W = 16
SSM_STATE = 64
RMS_EPS = 1e-6
PAST_LEN = 16384

LANES = 128
SUBLANES = 8
GROUPS_PER_SLAB = LANES // SSM_GROUP_W
SLAB_STATES = GROUPS_PER_SLAB * SSM_STATE
STEPS_PER_TILE = 32
HALO_STEPS = 16
X_SLOTS = 3
FF_CHUNK = 1024
W_CHUNK_COLS = 1024
W_CAST_ROWS = 32
VMEM_LIMIT_BYTES = 56 * 1024 * 1024

F32 = jnp.float32
BF16 = jnp.bfloat16


def _dot(a, b):
    return jnp.dot(a, b, preferred_element_type=F32)


def _sigmoid(x):
    return 1.0 / (1.0 + jnp.exp(-x))


def _rmsnorm(x, g):
    y = x * lax.rsqrt(jnp.mean(x * x, axis=-1, keepdims=True) + RMS_EPS)
    return y * g


def _full_spec(shape):
    return pl.BlockSpec(shape, lambda i: (0,) * len(shape))


def _discretize(a_re, a_im, dt):
    mag = jnp.exp(a_re * dt)
    ang = a_im * dt
    lam_re = mag * jnp.cos(ang)
    lam_im = mag * jnp.sin(ang)
    p = lam_re - 1.0
    q = lam_im
    den = a_re * a_re + a_im * a_im
    coef_re = (p * a_re + q * a_im) / den
    coef_im = (q * a_re - p * a_im) / den
    return lam_re, lam_im, coef_re, coef_im


def _tile_states(x):
    x2 = jnp.concatenate([x, x], axis=-1)
    return jnp.concatenate([x2] * (SLAB_STATES // (2 * SSM_STATE)), axis=-1)


def _scale_state_cols(m, lr, li):
    re, im = m[:, :SLAB_STATES], m[:, SLAB_STATES:]
    return jnp.concatenate([lr * re - li * im, lr * im + li * re], axis=-1)


def _scale_state_cols_ct(ct, lr, li):
    re, m = ct[:, :SLAB_STATES], ct[:, SLAB_STATES:]
    return jnp.concatenate([lr * re + li * m, lr * m - li * re], axis=-1)


def _ssm_prep_kernel(a_re_ref, a_im_ref, log_dt_ref, b_re_ref, b_im_ref, c_re_ref, c_im_ref,
                     lam_ref, lam2_ref, bc_ref, cc_ref, d_ref):
    groups, n = a_re_ref.shape
    slabs = bc_ref.shape[0]
    chans = b_re_ref.shape[0] // groups
    dt = jnp.exp(log_dt_ref[...])
    a_re = a_re_ref[...]
    a_im = a_im_ref[...]

    lr, li, _, _ = _discretize(_tile_states(a_re), _tile_states(a_im), dt)
    row = lax.broadcasted_iota(jnp.int32, (groups, SLAB_STATES), 0)
    col = lax.broadcasted_iota(jnp.int32, (groups, SLAB_STATES), 1)
    own = (row % GROUPS_PER_SLAB) == (col // n)
    lr = jnp.where(own, lr, 0.0).reshape(slabs, GROUPS_PER_SLAB, SLAB_STATES).sum(axis=1)
    li = jnp.where(own, li, 0.0).reshape(slabs, GROUPS_PER_SLAB, SLAB_STATES).sum(axis=1)
    lam_ref[0] = lr
    lam_ref[1] = li
    lam2_ref[0] = lr * lr - li * li
    lam2_ref[1] = 2.0 * lr * li

    rep = lambda a: jnp.broadcast_to(a[:, None, :], (groups, chans, a.shape[-1])).reshape(
        groups * chans, a.shape[-1])
    _, _, coef_re, coef_im = _discretize(rep(a_re), rep(a_im), rep(dt))
    b_re = b_re_ref[...]
    b_im = b_im_ref[...]
    bbar_re = coef_re * b_re - coef_im * b_im
    bbar_im = coef_re * b_im + coef_im * b_re

    r2 = lax.broadcasted_iota(jnp.int32, (LANES, SLAB_STATES), 0)
    c2 = lax.broadcasted_iota(jnp.int32, (LANES, SLAB_STATES), 1)
    diag = (r2 // chans) == (c2 // n)
    block_diag = lambda m: jnp.where(diag, _tile_states(m), 0.0)
    zero = jnp.zeros((LANES, LANES), F32)
    contract_states = (((1,), (1,)), ((), ()))
    for j in range(slabs):
        rows = slice(LANES * j, LANES * (j + 1))
        lrj, lij = lr[j:j + 1, :], li[j:j + 1, :]
        b = jnp.concatenate([block_diag(bbar_re[rows]), block_diag(bbar_im[rows])], axis=-1)
        ct = jnp.concatenate([block_diag(c_re_ref[rows, :]), -block_diag(c_im_ref[rows, :])], axis=-1)
        lam_b = _scale_state_cols(b, lrj, lij)
        c_lam_t = _scale_state_cols_ct(ct, lrj, lij)
        c_lam2_t = _scale_state_cols_ct(c_lam_t, lrj, lij)
        bc_ref[j] = jnp.concatenate([lam_b, b], axis=0).astype(BF16)
        cc_ref[j] = jnp.concatenate([c_lam_t, c_lam2_t], axis=0).T.astype(BF16)
        cb = lax.dot_general(b, ct, contract_states, precision=lax.Precision.HIGHEST,
                             preferred_element_type=F32)
        c_lam_b = lax.dot_general(lam_b, ct, contract_states, precision=lax.Precision.HIGHEST,
                                  preferred_element_type=F32)
        d_ref[j] = jnp.concatenate([jnp.concatenate([cb, c_lam_b], axis=-1),
                                    jnp.concatenate([zero, cb], axis=-1)], axis=0).astype(BF16)


def _ssm_prep(a_re, a_im, log_dt, b_re, b_im, c_re, c_im):
    g, n = a_re.shape
    c = b_re.shape[-1]
    slabs = g // GROUPS_PER_SLAB
    ins = (a_re, a_im, log_dt.reshape(g, 1),
           b_re.transpose(0, 2, 1).reshape(g * c, n), b_im.transpose(0, 2, 1).reshape(g * c, n),
           c_re.reshape(g * c, n), c_im.reshape(g * c, n))
    outs = (jax.ShapeDtypeStruct((2, slabs, SLAB_STATES), F32),
            jax.ShapeDtypeStruct((2, slabs, SLAB_STATES), F32),
            jax.ShapeDtypeStruct((slabs, 2 * LANES, 2 * SLAB_STATES), BF16),
            jax.ShapeDtypeStruct((slabs, 2 * SLAB_STATES, 2 * LANES), BF16),
            jax.ShapeDtypeStruct((slabs, 2 * LANES, 2 * LANES), BF16))
    return pl.pallas_call(
        _ssm_prep_kernel,
        out_shape=outs,
        grid=(1,),
        in_specs=[_full_spec(x.shape) for x in ins],
        out_specs=[_full_spec(o.shape) for o in outs],
        name="ssm_prep",
    )(*ins)


def _pool_hist_kernel(cache_ref, hist_ref):
    for g, w in enumerate(POOL_WINDOWS):
        cols = slice(g * LANES, (g + 1) * LANES)
        acc = cache_ref[POOL_BUF - 1, :, cols]
        for s in range(2, w):
            acc = acc + cache_ref[POOL_BUF - s, :, cols]
        hist_ref[:, cols] = acc


def _pool_hist(cache_tm):
    _, n, width = cache_tm.shape
    return pl.pallas_call(
        _pool_hist_kernel,
        out_shape=jax.ShapeDtypeStruct((n, width), F32),
        grid=(1,),
        in_specs=[_full_spec(cache_tm.shape)],
        out_specs=_full_spec((n, width)),
        name="pool_hist",
    )(cache_tm)


def _pool_project(pooled, pool_w_ref, pool_scale_ref, pool_out_ref):
    mixed = [_dot(p.astype(BF16), pool_w_ref[g * LANES:(g + 1) * LANES, :])
             for g, p in enumerate(pooled)]
    yp = jnp.concatenate(mixed, axis=-1) * pool_scale_ref[...]
    return _dot(yp.astype(BF16), pool_out_ref[...])


def _finish_layer(x, h1, branch_pool, ys, w_in_ref, glu_ref, w_out_ref, g2_ref,
                  w_up_ref, w_down_ref):
    d = x.shape[-1]
    z = _dot(ys.astype(BF16), glu_ref[...])
    branch_ssm = z[:, :d] * _sigmoid(z[:, d:])
    gate_off = w_in_ref.shape[1] - 2 * d
    g_pool = _sigmoid(_dot(h1, w_in_ref[:, gate_off:gate_off + d]))
    g_ssm = _sigmoid(_dot(h1, w_in_ref[:, gate_off + d:]))
    merged = g_pool * branch_pool + g_ssm * branch_ssm
    x1 = x + _dot(merged.astype(BF16), w_out_ref[...])
    h2 = _rmsnorm(x1, g2_ref[...]).astype(BF16)
    d_ff = w_up_ref.shape[1]
    acc = None
    for c0 in range(0, d_ff, FF_CHUNK):
        a = jnp.maximum(_dot(h2, w_up_ref[:, c0:c0 + FF_CHUNK]), 0.0)
        part = _dot((a * a).astype(BF16), w_down_ref[c0:c0 + FF_CHUNK, :])
        acc = part if acc is None else acc + part
    return x1 + acc


def _load_weights_as_bf16(pairs, stage, sem):
    rows, stage_cols = stage.shape
    ring = stage_cols // W_CHUNK_COLS
    chunks = []
    for w_hbm, w_vm in pairs:
        k, n = w_hbm.shape
        cols = min(n, W_CHUNK_COLS)
        for r0 in range(0, k, rows):
            for c0 in range(0, n, cols):
                chunks.append((w_hbm, w_vm, r0, c0, cols))

    def copy(idx):
        w_hbm, _, r0, c0, cols = chunks[idx]
        sl = idx % ring
        return pltpu.make_async_copy(
            w_hbm.at[pl.ds(r0, rows), pl.ds(c0, cols)],
            stage.at[:, pl.ds(sl * W_CHUNK_COLS, cols)], sem.at[sl])

    for idx in range(min(ring, len(chunks))):
        copy(idx).start()
    for idx, (_, w_vm, r0, c0, cols) in enumerate(chunks):
        sl = idx % ring
        copy(idx).wait()

        def cast_rows(g, carry, w_vm=w_vm, r0=r0, c0=c0, cols=cols, sl=sl):
            r = pl.multiple_of(g * W_CAST_ROWS, W_CAST_ROWS)
            w_vm[pl.ds(r0 + r, W_CAST_ROWS), c0:c0 + cols] = stage[
                pl.ds(r, W_CAST_ROWS), sl * W_CHUNK_COLS:sl * W_CHUNK_COLS + cols].astype(BF16)
            return carry

        lax.fori_loop(0, rows // W_CAST_ROWS, cast_rows, 0)
        if idx + ring < len(chunks):
            copy(idx + ring).start()


def _layer_kernel(n_tiles,
                  x_hbm, w_in_hbm, pool_w_hbm, pool_out_hbm, glu_hbm, w_out_hbm, w_up_hbm, w_down_hbm,
                  g1_ref, pool_scale_ref, lam_ref, lam2_ref, bc_ref, cc_ref, d_ref, d_skip_ref,
                  g2_ref, gf_ref,
                  xs_ref, hist_ref, h0t_re_ref, h0t_im_ref, cache_hbm,
                  y_hbm, pool_s_hbm, pool_tail_ref, h_fin_ref, ys_out_ref, hst_re_ref, hst_im_ref,
                  xbuf, ybuf, sem_in, sem_out, sem_w, sem_pool, h1buf, x2buf, vbuf, s_buf, h_state, vs_buf,
                  w_in_ref, pool_w_ref, pool_out_ref, glu_ref, w_out_ref, w_up_ref, w_down_ref):
    i = pl.program_id(0)
    n_x, steps, batch, d_model = xbuf.shape
    rows = steps * batch
    halo = HALO_STEPS * batch
    n_slabs = bc_ref.shape[0]
    pool_width = vbuf.shape[1]

    def in_copy(tile, b):
        sl = lax.rem(tile, n_x)
        return pltpu.make_async_copy(x_hbm.at[b, pl.ds(tile * steps, steps), :],
                                     xbuf.at[sl, :, b, :], sem_in.at[sl, b])

    def out_copy(tile, b):
        sl = lax.rem(tile, 2)
        return pltpu.make_async_copy(ybuf.at[sl, :, b, :],
                                     y_hbm.at[b, pl.ds(tile * steps, steps), :], sem_out.at[sl, b])

    def final_norm_to_ybuf(sl):
        y = _rmsnorm(x2buf[sl], gf_ref[...])
        ybuf[sl] = y.reshape(steps, batch, d_model)

    def cache_shift_copy():
        return pltpu.make_async_copy(cache_hbm.at[pl.ds(1, POOL_BUF - 1)],
                                     pool_s_hbm.at[pl.ds(0, POOL_BUF - 1)], sem_pool.at[0])

    def cache_new_row_copy():
        return pltpu.make_async_copy(vs_buf, pool_s_hbm.at[POOL_BUF - 1], sem_pool.at[1])

    def ssm_cols(j):
        re_cols = slice(2 * SLAB_STATES * j, 2 * SLAB_STATES * j + SLAB_STATES)
        im_cols = slice(2 * SLAB_STATES * j + SLAB_STATES, 2 * SLAB_STATES * (j + 1))
        return slice(SLAB_STATES * j, SLAB_STATES * (j + 1)), re_cols, im_cols

    @pl.when(i == 0)
    def _():
        for b in range(batch):
            in_copy(0, b).start()
        for b in range(batch):
            in_copy(1, b).start()
        cache_shift_copy().start()
        _load_weights_as_bf16(
            [(w_in_hbm, w_in_ref), (pool_w_hbm, pool_w_ref), (pool_out_hbm, pool_out_ref),
             (glu_hbm, glu_ref), (w_out_hbm, w_out_ref), (w_up_hbm, w_up_ref),
             (w_down_hbm, w_down_ref)], s_buf, sem_w)
        vbuf[0:halo, :] = jnp.zeros((halo, pool_width), F32)
        h_state[...] = jnp.zeros(h_state.shape, F32)
        x2buf[1] = jnp.zeros(x2buf.shape[1:], F32)
        for b in range(batch):
            in_copy(0, b).wait()
        h1buf[0] = _rmsnorm(xbuf[0].reshape(rows, d_model), g1_ref[...]).astype(BF16)

    @pl.when(i < n_tiles)
    def _prompt_step():
        xslot = lax.rem(i, n_x)
        xslot_next = lax.rem(i + 1, n_x)
        par = lax.rem(i, 2)

        @pl.when(i + 2 < n_tiles)
        def _():
            for b in range(batch):
                in_copy(i + 2, b).start()

        @pl.when(i + 1 < n_tiles)
        def _():
            for b in range(batch):
                in_copy(i + 1, b).wait()

        @pl.when(i >= 3)
        def _():
            for b in range(batch):
                out_copy(i - 3, b).wait()

        def tile_block(par):
            final_norm_to_ybuf(1 - par)
            h1buf[1 - par] = _rmsnorm(xbuf[xslot_next].reshape(rows, d_model), g1_ref[...]).astype(BF16)

            x = xbuf[xslot].reshape(rows, d_model)
            h1 = h1buf[par]
            v = _dot(h1, w_in_ref[:, 0:pool_width])
            vbuf[halo:halo + rows, :] = v
            u = _dot(h1, w_in_ref[:, pool_width:2 * pool_width])

            row = lax.broadcasted_iota(jnp.int32, (rows, LANES), 0)
            pos = i * steps + lax.shift_right_logical(row, int(math.log2(batch))) + 1
            pooled = []
            for g, w in enumerate(POOL_WINDOWS):
                cols = slice(g * LANES, (g + 1) * LANES)
                cur = vbuf[halo:halo + rows, cols]
                win = cur
                for s in range(1, w):
                    win = win + vbuf[halo - s * batch:halo - s * batch + rows, cols]
                cnt = jnp.minimum(w, pos).astype(F32)
                pooled.append(win / cnt - cur)
            branch_pool = _pool_project(pooled, pool_w_ref, pool_scale_ref, pool_out_ref)

            pairs = steps // 2
            half = pairs * batch
            ssm_width = u.shape[1]
            u4 = u.reshape(pairs, 2, batch, ssm_width)
            u_even = u4[:, 0].reshape(half, ssm_width)
            u_odd = u4[:, 1].reshape(half, ssm_width)
            u_pair = [jnp.concatenate([u_even[:, LANES * j:LANES * (j + 1)],
                                       u_odd[:, LANES * j:LANES * (j + 1)]], axis=-1).astype(BF16)
                      for j in range(n_slabs)]
            for j in range(n_slabs):
                s_buf[0:half, 2 * SLAB_STATES * j:2 * SLAB_STATES * (j + 1)] = _dot(u_pair[j], bc_ref[j])
            for j in range(n_slabs):
                _, re_cols, im_cols = ssm_cols(j)
                lr2 = jnp.broadcast_to(lam2_ref[0, j:j + 1, :], (batch, SLAB_STATES))
                li2 = jnp.broadcast_to(lam2_ref[1, j:j + 1, :], (batch, SLAB_STATES))
                hr = h_state[:, re_cols]
                hi = h_state[:, im_cols]
                for p in range(pairs):
                    r = slice(p * batch, (p + 1) * batch)
                    sr = s_buf[r, re_cols]
                    si = s_buf[r, im_cols]
                    s_buf[r, re_cols] = hr
                    s_buf[r, im_cols] = hi
                    hr, hi = lr2 * hr - li2 * hi + sr, lr2 * hi + li2 * hr + si
                h_state[:, re_cols] = hr
                h_state[:, im_cols] = hi
            y_pair = [_dot(s_buf[0:half, 2 * SLAB_STATES * j:2 * SLAB_STATES * (j + 1)].astype(BF16),
                           cc_ref[j]) + _dot(u_pair[j], d_ref[j]) for j in range(n_slabs)]
            y_even = jnp.concatenate([y[:, :LANES] for y in y_pair], axis=-1)
            y_odd = jnp.concatenate([y[:, LANES:] for y in y_pair], axis=-1)
            ys = jnp.concatenate([y_even.reshape(pairs, 1, batch, ssm_width),
                                  y_odd.reshape(pairs, 1, batch, ssm_width)], axis=1).reshape(rows, ssm_width)
            ys = ys + d_skip_ref[...] * u

            x2buf[par] = _finish_layer(x, h1, branch_pool, ys, w_in_ref, glu_ref, w_out_ref, g2_ref,
                                       w_up_ref, w_down_ref)

            tail = vbuf[rows:rows + halo, :]
            vbuf[0:halo, :] = tail
            pool_tail_ref[...] = tail[halo - POOL_BUF * batch:, :]
            h_fin_ref[...] = h_state[...]

        for parity in (0, 1):
            pl.when(par == parity)(functools.partial(tile_block, parity))

        @pl.when(i >= 1)
        def _():
            for b in range(batch):
                out_copy(i - 1, b).start()

        @pl.when(i == n_tiles - 1)
        def _():
            for b in range(batch):
                out_copy(i - 2, b).wait()
            final_norm_to_ybuf(par)
            for b in range(batch):
                out_copy(i, b).start()
            for b in range(batch):
                out_copy(i - 1, b).wait()
            for b in range(batch):
                out_copy(i, b).wait()

    @pl.when(i == n_tiles)
    def _decode_step():
        x = xs_ref[...]
        h1 = _rmsnorm(x, g1_ref[...]).astype(BF16)
        v = _dot(h1, w_in_ref[:, 0:pool_width])
        vs_buf[...] = v
        cache_new_row_copy().start()
        u = _dot(h1, w_in_ref[:, pool_width:2 * pool_width])

        pooled = []
        for g, w in enumerate(POOL_WINDOWS):
            cols = slice(g * LANES, (g + 1) * LANES)
            cur = v[:, cols]
            cnt = float(min(w, PAST_LEN + 1))
            pooled.append((hist_ref[:, cols] + cur) / cnt - cur)
        branch_pool = _pool_project(pooled, pool_w_ref, pool_scale_ref, pool_out_ref)

        u_bf = u.astype(BF16)
        ys_parts = []
        for j in range(n_slabs):
            st, _, _ = ssm_cols(j)
            u_j = u_bf[:, LANES * j:LANES * (j + 1)]
            bu = _dot(u_j, bc_ref[j, LANES:2 * LANES, :])
            lr = lam_ref[0, j:j + 1, :]
            li = lam_ref[1, j:j + 1, :]
            hr = h0t_re_ref[st, :].T
            hi = h0t_im_ref[st, :].T
            nr = lr * hr - li * hi + bu[:, :SLAB_STATES]
            ni = lr * hi + li * hr + bu[:, SLAB_STATES:]
            hst_re_ref[st, :] = nr.T
            hst_im_ref[st, :] = ni.T
            h_prev = jnp.concatenate([hr, hi], axis=-1).astype(BF16)
            ys_parts.append(_dot(h_prev, cc_ref[j, :, 0:LANES]) + _dot(u_j, d_ref[j, 0:LANES, 0:LANES]))
        ys = jnp.concatenate(ys_parts, axis=-1) + d_skip_ref[...] * u

        x2 = _finish_layer(x, h1, branch_pool, ys, w_in_ref, glu_ref, w_out_ref, g2_ref,
                           w_up_ref, w_down_ref)
        ys_out_ref[...] = _rmsnorm(x2, gf_ref[...])
        cache_shift_copy().wait()
        cache_new_row_copy().wait()


def _const_spec(shape):
    return pl.BlockSpec(shape, lambda i: (0,) * len(shape), pipeline_mode=pl.Buffered(1))


def _layer_call(x, big_weights, small_params, sample_ins):
    batch, seq, d = x.shape
    steps = STEPS_PER_TILE
    rows = steps * batch
    n_tiles = seq // steps
    assert n_tiles >= X_SLOTS, "the input ring and the deferred final norm assume at least 3 tiles"
    xs, hist, h0t_re, _, cache_tm = sample_ins
    dec_rows = xs.shape[0]
    pool_width = hist.shape[-1]
    n_states = h0t_re.shape[0]
    halo = HALO_STEPS * batch
    assert all(w.shape[0] % rows == 0 and w.shape[1] % min(w.shape[1], W_CHUNK_COLS) == 0
               for w in big_weights), "weight staging chunks must tile every weight"
    out_shape = (jax.ShapeDtypeStruct((batch, seq, d), F32),
                 jax.ShapeDtypeStruct(cache_tm.shape, F32),
                 jax.ShapeDtypeStruct((POOL_BUF * batch, pool_width), F32),
                 jax.ShapeDtypeStruct((batch, 2 * n_states), F32),
                 jax.ShapeDtypeStruct((dec_rows, d), F32),
                 jax.ShapeDtypeStruct((n_states, dec_rows), F32),
                 jax.ShapeDtypeStruct((n_states, dec_rows), F32))
    any_spec = pl.BlockSpec(memory_space=pl.ANY)
    return pl.pallas_call(
        functools.partial(_layer_kernel, n_tiles),
        out_shape=out_shape,
        grid=(n_tiles + 1,),
        in_specs=([any_spec] * (1 + len(big_weights))
                  + [_const_spec(p.shape) for p in small_params]
                  + [_const_spec(a.shape) for a in sample_ins[:-1]] + [any_spec]),
        out_specs=[any_spec, any_spec] + [_full_spec(o.shape) for o in out_shape[2:]],
        scratch_shapes=[pltpu.VMEM((X_SLOTS, steps, batch, d), F32),
                        pltpu.VMEM((2, steps, batch, d), F32),
                        pltpu.SemaphoreType.DMA((X_SLOTS, batch)),
                        pltpu.SemaphoreType.DMA((2, batch)),
                        pltpu.SemaphoreType.DMA((2 * n_states // W_CHUNK_COLS,)),
                        pltpu.SemaphoreType.DMA((2,)),
                        pltpu.VMEM((2, rows, d), BF16),
                        pltpu.VMEM((2, rows, d), F32),
                        pltpu.VMEM((halo + rows, pool_width), F32),
                        pltpu.VMEM((rows, 2 * n_states), F32),
                        pltpu.VMEM((batch, 2 * n_states), F32),
                        pltpu.VMEM((dec_rows, pool_width), F32)]
                       + [pltpu.VMEM(w.shape, BF16) for w in big_weights],
        compiler_params=pltpu.CompilerParams(dimension_semantics=("arbitrary",),
                                             vmem_limit_bytes=VMEM_LIMIT_BYTES),
        name="layer",
    )(x, *big_weights, *small_params, *sample_ins)


def kernel(x_prompt, x_sample, cache_pool, state_ssm_re, state_ssm_im, norm1_g, w_in, pool_w,
           pool_scale, pool_out, ssm_a_re, ssm_a_im, ssm_log_dt, ssm_b_re, ssm_b_im, ssm_c_re,
           ssm_c_im, ssm_d, ssm_glu, w_out, norm2_g, w_up, w_down, normf_g):
    depth = w_in.shape[0]
    assert depth == 1, "the final norm is fused into the (single) layer kernel"
    batch, seq, d = x_prompt.shape
    dec_batch, dec_seq, _ = x_sample.shape
    assert dec_seq == 1 and batch == SUBLANES and seq % STEPS_PER_TILE == 0
    groups, n_state = ssm_a_re.shape[1:]
    n_states = groups * n_state
    l = 0

    ssm_operands = _ssm_prep(ssm_a_re[l], ssm_a_im[l], ssm_log_dt[l], ssm_b_re[l], ssm_b_im[l],
                             ssm_c_re[l], ssm_c_im[l])
    row = lambda a: a.reshape(1, -1)
    pool_groups, pool_gw, _ = pool_w[l].shape
    big_weights = (w_in[l], pool_w[l].reshape(pool_groups * pool_gw, pool_gw), pool_out[l],
                   ssm_glu[l], w_out[l], w_up[l], w_down[l])
    small_params = (row(norm1_g[l]), row(pool_scale[l]), *ssm_operands,
                    row(ssm_d[l]), row(norm2_g[l]), row(normf_g))
    cache_tm = cache_pool[l].transpose(1, 0, 2)
    state_t = lambda s: s.transpose(1, 2, 0).reshape(n_states, dec_batch)
    sample_ins = (x_sample.reshape(dec_batch, d), _pool_hist(cache_tm),
                  state_t(state_ssm_re[l]), state_t(state_ssm_im[l]), cache_tm)

    y_prompt, pool_s_tm, pool_tail, h_fin, y_s, re_s_t, im_s_t = _layer_call(
        x_prompt, big_weights, small_params, sample_ins)

    pool_width = pool_tail.shape[-1]
    pool_p = pool_tail.reshape(POOL_BUF, batch, pool_width).transpose(1, 0, 2)
    h_fin = h_fin.reshape(batch, n_states // SLAB_STATES, 2, SLAB_STATES)
    re_p = h_fin[:, :, 0].reshape(batch, groups, n_state)
    im_p = h_fin[:, :, 1].reshape(batch, groups, n_state)
    pool_s = pool_s_tm.transpose(1, 0, 2)
    state_back = lambda s: s.reshape(groups, n_state, dec_batch).transpose(2, 0, 1)

    return (y_prompt, y_s.reshape(dec_batch, dec_seq, d),
            pool_p[None], re_p[None], im_p[None],
            pool_s[None], state_back(re_s_t)[None], state_back(im_s_t)[None])
```

```python
import functools
import math

import jax
import jax.numpy as jnp
from jax import lax
from jax.experimental import pallas as pl
from jax.experimental.pallas import tpu as pltpu

POOL_WINDOWS = (2, 4, 8, 16)
POOL_BUF = max(POOL_WINDOWS) - 1
SSM_GROUP_W = 16
SSM_STATE = 64
RMS_EPS = 1e-6
PAST_LEN = 16384

LANES = 128
SUBLANES = 8
GROUPS_PER_SLAB = LANES // SSM_GROUP_W
SLAB_STATES = GROUPS_PER_SLAB * SSM_STATE
STEPS_PER_TILE = 32
HALO_STEPS = 16
X_SLOTS = 3
FF_CHUNK = 1024
W_CHUNK_COLS = 1024
W_CAST_ROWS = 32
VMEM_LIMIT_BYTES = 56 * 1024 * 1024

F32 = jnp.float32
BF16 = jnp.bfloat16


def _dot(a, b):
    return jnp.dot(a, b, preferred_element_type=F32)


def _sigmoid(x):
    return 1.0 / (1.0 + jnp.exp(-x))


def _rmsnorm(x, g):
    y = x * lax.rsqrt(jnp.mean(x * x, axis=-1, keepdims=True) + RMS_EPS)
    return y * g


def _full_spec(shape):
    return pl.BlockSpec(shape, lambda i: (0,) * len(shape))


def _discretize(a_re, a_im, dt):
    mag = jnp.exp(a_re * dt)
    ang = a_im * dt
    lam_re = mag * jnp.cos(ang)
    lam_im = mag * jnp.sin(ang)
    p = lam_re - 1.0
    q = lam_im
    den = a_re * a_re + a_im * a_im
    coef_re = (p * a_re + q * a_im) / den
    coef_im = (q * a_re - p * a_im) / den
    return lam_re, lam_im, coef_re, coef_im


def _tile_states(x):
    x2 = jnp.concatenate([x, x], axis=-1)
    return jnp.concatenate([x2] * (SLAB_STATES // (2 * SSM_STATE)), axis=-1)


def _scale_state_cols(m, lr, li):
    re, im = m[:, :SLAB_STATES], m[:, SLAB_STATES:]
    return jnp.concatenate([lr * re - li * im, lr * im + li * re], axis=-1)


def _scale_state_cols_ct(ct, lr, li):
    re, m = ct[:, :SLAB_STATES], ct[:, SLAB_STATES:]
    return jnp.concatenate([lr * re + li * m, lr * m - li * re], axis=-1)


def _ssm_prep_kernel(a_re_ref, a_im_ref, log_dt_ref, b_re_ref, b_im_ref, c_re_ref, c_im_ref,
                     lam_ref, lam2_ref, bc_ref, cc_ref, d_ref):
    groups, n = a_re_ref.shape
    slabs = bc_ref.shape[0]
    chans = b_re_ref.shape[0] // groups
    dt = jnp.exp(log_dt_ref[...])
    a_re = a_re_ref[...]
    a_im = a_im_ref[...]

    lr, li, _, _ = _discretize(_tile_states(a_re), _tile_states(a_im), dt)
    row = lax.broadcasted_iota(jnp.int32, (groups, SLAB_STATES), 0)
    col = lax.broadcasted_iota(jnp.int32, (groups, SLAB_STATES), 1)
    own = (row % GROUPS_PER_SLAB) == (col // n)
    lr = jnp.where(own, lr, 0.0).reshape(slabs, GROUPS_PER_SLAB, SLAB_STATES).sum(axis=1)
    li = jnp.where(own, li, 0.0).reshape(slabs, GROUPS_PER_SLAB, SLAB_STATES).sum(axis=1)
    lam_ref[0] = lr
    lam_ref[1] = li
    lam2_ref[0] = lr * lr - li * li
    lam2_ref[1] = 2.0 * lr * li

    rep = lambda a: jnp.broadcast_to(a[:, None, :], (groups, chans, a.shape[-1])).reshape(
        groups * chans, a.shape[-1])
    _, _, coef_re, coef_im = _discretize(rep(a_re), rep(a_im), rep(dt))
    b_re = b_re_ref[...]
    b_im = b_im_ref[...]
    bbar_re = coef_re * b_re - coef_im * b_im
    bbar_im = coef_re * b_im + coef_im * b_re

    r2 = lax.broadcasted_iota(jnp.int32, (LANES, SLAB_STATES), 0)
    c2 = lax.broadcasted_iota(jnp.int32, (LANES, SLAB_STATES), 1)
    diag = (r2 // chans) == (c2 // n)
    block_diag = lambda m: jnp.where(diag, _tile_states(m), 0.0)
    zero = jnp.zeros((LANES, LANES), F32)
    contract_states = (((1,), (1,)), ((), ()))
    for j in range(slabs):
        rows = slice(LANES * j, LANES * (j + 1))
        lrj, lij = lr[j:j + 1, :], li[j:j + 1, :]
        b = jnp.concatenate([block_diag(bbar_re[rows]), block_diag(bbar_im[rows])], axis=-1)
        ct = jnp.concatenate([block_diag(c_re_ref[rows, :]), -block_diag(c_im_ref[rows, :])], axis=-1)
        lam_b = _scale_state_cols(b, lrj, lij)
        c_lam_t = _scale_state_cols_ct(ct, lrj, lij)
        c_lam2_t = _scale_state_cols_ct(c_lam_t, lrj, lij)
        bc_ref[j] = jnp.concatenate([lam_b, b], axis=0).astype(BF16)
        cc_ref[j] = jnp.concatenate([c_lam_t, c_lam2_t], axis=0).T.astype(BF16)
        cb = lax.dot_general(b, ct, contract_states, precision=lax.Precision.HIGHEST,
                             preferred_element_type=F32)
        c_lam_b = lax.dot_general(lam_b, ct, contract_states, precision=lax.Precision.HIGHEST,
                                  preferred_element_type=F32)
        d_ref[j] = jnp.concatenate([jnp.concatenate([cb, c_lam_b], axis=-1),
                                    jnp.concatenate([zero, cb], axis=-1)], axis=0).astype(BF16)


def _ssm_prep(a_re, a_im, log_dt, b_re, b_im, c_re, c_im):
    g, n = a_re.shape
    c = b_re.shape[-1]
    slabs = g // GROUPS_PER_SLAB
    ins = (a_re, a_im, log_dt.reshape(g, 1),
           b_re.transpose(0, 2, 1).reshape(g * c, n), b_im.transpose(0, 2, 1).reshape(g * c, n),
           c_re.reshape(g * c, n), c_im.reshape(g * c, n))
    outs = (jax.ShapeDtypeStruct((2, slabs, SLAB_STATES), F32),
            jax.ShapeDtypeStruct((2, slabs, SLAB_STATES), F32),
            jax.ShapeDtypeStruct((slabs, 2 * LANES, 2 * SLAB_STATES), BF16),
            jax.ShapeDtypeStruct((slabs, 2 * SLAB_STATES, 2 * LANES), BF16),
            jax.ShapeDtypeStruct((slabs, 2 * LANES, 2 * LANES), BF16))
    return pl.pallas_call(
        _ssm_prep_kernel,
        out_shape=outs,
        grid=(1,),
        in_specs=[_full_spec(x.shape) for x in ins],
        out_specs=[_full_spec(o.shape) for o in outs],
        name="ssm_prep",
    )(*ins)


def _pool_hist_kernel(cache_ref, hist_ref):
    for g, w in enumerate(POOL_WINDOWS):
        cols = slice(g * LANES, (g + 1) * LANES)
        acc = cache_ref[POOL_BUF - 1, :, cols]
        for s in range(2, w):
            acc = acc + cache_ref[POOL_BUF - s, :, cols]
        hist_ref[:, cols] = acc


def _pool_hist(cache_tm):
    _, n, width = cache_tm.shape
    return pl.pallas_call(
        _pool_hist_kernel,
        out_shape=jax.ShapeDtypeStruct((n, width), F32),
        grid=(1,),
        in_specs=[_full_spec(cache_tm.shape)],
        out_specs=_full_spec((n, width)),
        name="pool_hist",
    )(cache_tm)


def _pool_project(pooled, w_pool_ref):
    return _dot(jnp.concatenate(pooled, axis=-1).astype(BF16), w_pool_ref[...])


def _pool_fold_kernel(pool_w_ref, pool_scale_ref, pool_out_ref, w_pool_ref):
    for g in range(len(POOL_WINDOWS)):
        rows = slice(g * LANES, (g + 1) * LANES)
        scaled = pool_w_ref[rows, :] * pool_scale_ref[:, rows]
        w_pool_ref[rows, :] = jnp.dot(scaled, pool_out_ref[rows, :], precision=lax.Precision.HIGHEST,
                                      preferred_element_type=F32).astype(BF16)


def _pool_fold(pool_w, pool_scale, pool_out):
    groups, gw, _ = pool_w.shape
    ins = (pool_w.reshape(groups * gw, gw), pool_scale.reshape(1, -1), pool_out)
    return pl.pallas_call(
        _pool_fold_kernel,
        out_shape=jax.ShapeDtypeStruct(pool_out.shape, BF16),
        grid=(1,),
        in_specs=[_full_spec(a.shape) for a in ins],
        out_specs=_full_spec(pool_out.shape),
        name="pool_fold",
    )(*ins)


def _finish_layer(x, h1, branch_pool, ys, w_in_ref, glu_ref, w_out_ref, g2_ref,
                  w_up_ref, w_down_ref):
    d = x.shape[-1]
    z = _dot(ys.astype(BF16), glu_ref[...])
    branch_ssm = z[:, :d] * _sigmoid(z[:, d:])
    gate_off = w_in_ref.shape[1] - 2 * d
    g_pool = _sigmoid(_dot(h1, w_in_ref[:, gate_off:gate_off + d]))
    g_ssm = _sigmoid(_dot(h1, w_in_ref[:, gate_off + d:]))
    merged = g_pool * branch_pool + g_ssm * branch_ssm
    x1 = x + _dot(merged.astype(BF16), w_out_ref[...])
    h2 = _rmsnorm(x1, g2_ref[...]).astype(BF16)
    d_ff = w_up_ref.shape[1]
    acc = None
    for c0 in range(0, d_ff, FF_CHUNK):
        a = jnp.maximum(_dot(h2, w_up_ref[:, c0:c0 + FF_CHUNK]), 0.0)
        part = _dot((a * a).astype(BF16), w_down_ref[c0:c0 + FF_CHUNK, :])
        acc = part if acc is None else acc + part
    return x1 + acc


def _load_weights_as_bf16(pairs, stage, sem):
    rows, stage_cols = stage.shape
    ring = stage_cols // W_CHUNK_COLS
    chunks = []
    for w_hbm, w_vm in pairs:
        k, n = w_hbm.shape
        cols = min(n, W_CHUNK_COLS)
        for r0 in range(0, k, rows):
            for c0 in range(0, n, cols):
                chunks.append((w_hbm, w_vm, r0, c0, cols))

    def copy(idx):
        w_hbm, _, r0, c0, cols = chunks[idx]
        sl = idx % ring
        return pltpu.make_async_copy(
            w_hbm.at[pl.ds(r0, rows), pl.ds(c0, cols)],
            stage.at[:, pl.ds(sl * W_CHUNK_COLS, cols)], sem.at[sl])

    for idx in range(min(ring, len(chunks))):
        copy(idx).start()
    for idx, (_, w_vm, r0, c0, cols) in enumerate(chunks):
        sl = idx % ring
        copy(idx).wait()

        def cast_rows(g, carry, w_vm=w_vm, r0=r0, c0=c0, cols=cols, sl=sl):
            r = pl.multiple_of(g * W_CAST_ROWS, W_CAST_ROWS)
            w_vm[pl.ds(r0 + r, W_CAST_ROWS), c0:c0 + cols] = stage[
                pl.ds(r, W_CAST_ROWS), sl * W_CHUNK_COLS:sl * W_CHUNK_COLS + cols].astype(BF16)
            return carry

        lax.fori_loop(0, rows // W_CAST_ROWS, cast_rows, 0)
        if idx + ring < len(chunks):
            copy(idx + ring).start()


def _layer_kernel(n_tiles,
                  x_hbm, w_in_hbm, glu_hbm, w_out_hbm, w_up_hbm, w_down_hbm,
                  g1_ref, w_pool_ref, lam_ref, lam2_ref, bc_ref, cc_ref, d_ref, d_skip_ref,
                  g2_ref, gf_ref,
                  xs_ref, hist_ref, h0t_re_ref, h0t_im_ref, cache_hbm,
                  y_hbm, pool_s_hbm, pool_tail_ref, h_fin_ref, ys_out_ref, hst_re_ref, hst_im_ref,
                  xbuf, ybuf, sem_in, sem_out, sem_w, sem_pool, h1buf, x2buf, vbuf, s_buf, h_state, vs_buf,
                  w_in_ref, glu_ref, w_out_ref, w_up_ref, w_down_ref):
    i = pl.program_id(0)
    n_x, steps, batch, d_model = xbuf.shape
    rows = steps * batch
    halo = HALO_STEPS * batch
    n_slabs = bc_ref.shape[0]
    pool_width = vbuf.shape[1]

    def in_copy(tile, b):
        sl = lax.rem(tile, n_x)
        return pltpu.make_async_copy(x_hbm.at[b, pl.ds(tile * steps, steps), :],
                                     xbuf.at[sl, :, b, :], sem_in.at[sl, b])

    def out_copy(tile, b):
        sl = lax.rem(tile, 2)
        return pltpu.make_async_copy(ybuf.at[sl, :, b, :],
                                     y_hbm.at[b, pl.ds(tile * steps, steps), :], sem_out.at[sl, b])

    def final_norm_to_ybuf(sl):
        y = _rmsnorm(x2buf[sl], gf_ref[...])
        ybuf[sl] = y.reshape(steps, batch, d_model)

    def cache_shift_copy():
        return pltpu.make_async_copy(cache_hbm.at[pl.ds(1, POOL_BUF - 1)],
                                     pool_s_hbm.at[pl.ds(0, POOL_BUF - 1)], sem_pool.at[0])

    def cache_new_row_copy():
        return pltpu.make_async_copy(vs_buf, pool_s_hbm.at[POOL_BUF - 1], sem_pool.at[1])

    def ssm_cols(j):
        re_cols = slice(2 * SLAB_STATES * j, 2 * SLAB_STATES * j + SLAB_STATES)
        im_cols = slice(2 * SLAB_STATES * j + SLAB_STATES, 2 * SLAB_STATES * (j + 1))
        return slice(SLAB_STATES * j, SLAB_STATES * (j + 1)), re_cols, im_cols

    @pl.when(i == 0)
    def _():
        for b in range(batch):
            in_copy(0, b).start()
        for b in range(batch):
            in_copy(1, b).start()
        cache_shift_copy().start()
        _load_weights_as_bf16(
            [(w_in_hbm, w_in_ref),
             (glu_hbm, glu_ref), (w_out_hbm, w_out_ref), (w_up_hbm, w_up_ref),
             (w_down_hbm, w_down_ref)], s_buf, sem_w)
        vbuf[0:halo, :] = jnp.zeros((halo, pool_width), F32)
        h_state[...] = jnp.zeros(h_state.shape, F32)
        x2buf[1] = jnp.zeros(x2buf.shape[1:], F32)
        for b in range(batch):
            in_copy(0, b).wait()
        h1buf[0] = _rmsnorm(xbuf[0].reshape(rows, d_model), g1_ref[...]).astype(BF16)

    @pl.when(i < n_tiles)
    def _prompt_step():
        xslot = lax.rem(i, n_x)
        xslot_next = lax.rem(i + 1, n_x)
        par = lax.rem(i, 2)

        @pl.when(i + 2 < n_tiles)
        def _():
            for b in range(batch):
                in_copy(i + 2, b).start()

        @pl.when(i + 1 < n_tiles)
        def _():
            for b in range(batch):
                in_copy(i + 1, b).wait()

        @pl.when(i >= 3)
        def _():
            for b in range(batch):
                out_copy(i - 3, b).wait()

        def tile_block(par):
            final_norm_to_ybuf(1 - par)
            h1buf[1 - par] = _rmsnorm(xbuf[xslot_next].reshape(rows, d_model), g1_ref[...]).astype(BF16)

            x = xbuf[xslot].reshape(rows, d_model)
            h1 = h1buf[par]
            v = _dot(h1, w_in_ref[:, 0:pool_width])
            vbuf[halo:halo + rows, :] = v
            u = _dot(h1, w_in_ref[:, pool_width:2 * pool_width])

            row = lax.broadcasted_iota(jnp.int32, (rows, LANES), 0)
            pos = i * steps + lax.shift_right_logical(row, int(math.log2(batch))) + 1
            pooled = []
            for g, w in enumerate(POOL_WINDOWS):
                cols = slice(g * LANES, (g + 1) * LANES)
                cur = vbuf[halo:halo + rows, cols]
                win = cur
                for s in range(1, w):
                    win = win + vbuf[halo - s * batch:halo - s * batch + rows, cols]
                cnt = jnp.minimum(w, pos).astype(F32)
                pooled.append(win / cnt - cur)
            branch_pool = _pool_project(pooled, w_pool_ref)

            pairs = steps // 2
            half = pairs * batch
            ssm_width = u.shape[1]
            u4 = u.reshape(pairs, 2, batch, ssm_width)
            u_even = u4[:, 0].reshape(half, ssm_width)
            u_odd = u4[:, 1].reshape(half, ssm_width)
            u_pair = [jnp.concatenate([u_even[:, LANES * j:LANES * (j + 1)],
                                       u_odd[:, LANES * j:LANES * (j + 1)]], axis=-1).astype(BF16)
                      for j in range(n_slabs)]
            for j in range(n_slabs):
                s_buf[0:half, 2 * SLAB_STATES * j:2 * SLAB_STATES * (j + 1)] = _dot(u_pair[j], bc_ref[j])
            for j in range(n_slabs):
                _, re_cols, im_cols = ssm_cols(j)
                lr2 = jnp.broadcast_to(lam2_ref[0, j:j + 1, :], (batch, SLAB_STATES))
                li2 = jnp.broadcast_to(lam2_ref[1, j:j + 1, :], (batch, SLAB_STATES))
                hr = h_state[:, re_cols]
                hi = h_state[:, im_cols]
                for p in range(pairs):
                    r = slice(p * batch, (p + 1) * batch)
                    sr = s_buf[r, re_cols]
                    si = s_buf[r, im_cols]
                    s_buf[r, re_cols] = hr
                    s_buf[r, im_cols] = hi
                    hr, hi = lr2 * hr - li2 * hi + sr, lr2 * hi + li2 * hr + si
                h_state[:, re_cols] = hr
                h_state[:, im_cols] = hi
            y_pair = [_dot(s_buf[0:half, 2 * SLAB_STATES * j:2 * SLAB_STATES * (j + 1)].astype(BF16),
                           cc_ref[j]) + _dot(u_pair[j], d_ref[j]) for j in range(n_slabs)]
            y_even = jnp.concatenate([y[:, :LANES] for y in y_pair], axis=-1)
            y_odd = jnp.concatenate([y[:, LANES:] for y in y_pair], axis=-1)
            ys = jnp.concatenate([y_even.reshape(pairs, 1, batch, ssm_width),
                                  y_odd.reshape(pairs, 1, batch, ssm_width)], axis=1).reshape(rows, ssm_width)
            ys = ys + d_skip_ref[...] * u

            x2buf[par] = _finish_layer(x, h1, branch_pool, ys, w_in_ref, glu_ref, w_out_ref, g2_ref,
                                       w_up_ref, w_down_ref)

            tail = vbuf[rows:rows + halo, :]
            vbuf[0:halo, :] = tail
            pool_tail_ref[...] = tail[halo - POOL_BUF * batch:, :]
            h_fin_ref[...] = h_state[...]

        for parity in (0, 1):
            pl.when(par == parity)(functools.partial(tile_block, parity))

        @pl.when(i >= 1)
        def _():
            for b in range(batch):
                out_copy(i - 1, b).start()

        @pl.when(i == n_tiles - 1)
        def _():
            for b in range(batch):
                out_copy(i - 2, b).wait()
            final_norm_to_ybuf(par)
            for b in range(batch):
                out_copy(i, b).start()
            for b in range(batch):
                out_copy(i - 1, b).wait()
            for b in range(batch):
                out_copy(i, b).wait()

    @pl.when(i == n_tiles)
    def _decode_step():
        x = xs_ref[...]
        h1 = _rmsnorm(x, g1_ref[...]).astype(BF16)
        v = _dot(h1, w_in_ref[:, 0:pool_width])
        vs_buf[...] = v
        cache_new_row_copy().start()
        u = _dot(h1, w_in_ref[:, pool_width:2 * pool_width])

        pooled = []
        for g, w in enumerate(POOL_WINDOWS):
            cols = slice(g * LANES, (g + 1) * LANES)
            cur = v[:, cols]
            cnt = float(min(w, PAST_LEN + 1))
            pooled.append((hist_ref[:, cols] + cur) / cnt - cur)
        branch_pool = _pool_project(pooled, w_pool_ref)

        u_bf = u.astype(BF16)
        ys_parts = []
        for j in range(n_slabs):
            st, _, _ = ssm_cols(j)
            u_j = u_bf[:, LANES * j:LANES * (j + 1)]
            bu = _dot(u_j, bc_ref[j, LANES:2 * LANES, :])
            lr = lam_ref[0, j:j + 1, :]
            li = lam_ref[1, j:j + 1, :]
            hr = h0t_re_ref[st, :].T
            hi = h0t_im_ref[st, :].T
            nr = lr * hr - li * hi + bu[:, :SLAB_STATES]
            ni = lr * hi + li * hr + bu[:, SLAB_STATES:]
            hst_re_ref[st, :] = nr.T
            hst_im_ref[st, :] = ni.T
            h_prev = jnp.concatenate([hr, hi], axis=-1).astype(BF16)
            ys_parts.append(_dot(h_prev, cc_ref[j, :, 0:LANES]) + _dot(u_j, d_ref[j, 0:LANES, 0:LANES]))
        ys = jnp.concatenate(ys_parts, axis=-1) + d_skip_ref[...] * u

        x2 = _finish_layer(x, h1, branch_pool, ys, w_in_ref, glu_ref, w_out_ref, g2_ref,
                           w_up_ref, w_down_ref)
        ys_out_ref[...] = _rmsnorm(x2, gf_ref[...])
        cache_shift_copy().wait()
        cache_new_row_copy().wait()


def _const_spec(shape):
    return pl.BlockSpec(shape, lambda i: (0,) * len(shape), pipeline_mode=pl.Buffered(1))


def _layer_call(x, big_weights, small_params, sample_ins):
    batch, seq, d = x.shape
    steps = STEPS_PER_TILE
    rows = steps * batch
    n_tiles = seq // steps
    assert n_tiles >= X_SLOTS, "the input ring and the deferred final norm assume at least 3 tiles"
    xs, hist, h0t_re, _, cache_tm = sample_ins
    dec_rows = xs.shape[0]
    pool_width = hist.shape[-1]
    n_states = h0t_re.shape[0]
    halo = HALO_STEPS * batch
    assert all(w.shape[0] % rows == 0 and w.shape[1] % min(w.shape[1], W_CHUNK_COLS) == 0
               for w in big_weights), "weight staging chunks must tile every weight"
    out_shape = (jax.ShapeDtypeStruct((batch, seq, d), F32),
                 jax.ShapeDtypeStruct(cache_tm.shape, F32),
                 jax.ShapeDtypeStruct((POOL_BUF * batch, pool_width), F32),
                 jax.ShapeDtypeStruct((batch, 2 * n_states), F32),
                 jax.ShapeDtypeStruct((dec_rows, d), F32),
                 jax.ShapeDtypeStruct((n_states, dec_rows), F32),
                 jax.ShapeDtypeStruct((n_states, dec_rows), F32))
    any_spec = pl.BlockSpec(memory_space=pl.ANY)
    return pl.pallas_call(
        functools.partial(_layer_kernel, n_tiles),
        out_shape=out_shape,
        grid=(n_tiles + 1,),
        in_specs=([any_spec] * (1 + len(big_weights))
                  + [_const_spec(p.shape) for p in small_params]
                  + [_const_spec(a.shape) for a in sample_ins[:-1]] + [any_spec]),
        out_specs=[any_spec, any_spec] + [_full_spec(o.shape) for o in out_shape[2:]],
        scratch_shapes=[pltpu.VMEM((X_SLOTS, steps, batch, d), F32),
                        pltpu.VMEM((2, steps, batch, d), F32),
                        pltpu.SemaphoreType.DMA((X_SLOTS, batch)),
                        pltpu.SemaphoreType.DMA((2, batch)),
                        pltpu.SemaphoreType.DMA((2 * n_states // W_CHUNK_COLS,)),
                        pltpu.SemaphoreType.DMA((2,)),
                        pltpu.VMEM((2, rows, d), BF16),
                        pltpu.VMEM((2, rows, d), F32),
                        pltpu.VMEM((halo + rows, pool_width), F32),
                        pltpu.VMEM((rows, 2 * n_states), F32),
                        pltpu.VMEM((batch, 2 * n_states), F32),
                        pltpu.VMEM((dec_rows, pool_width), F32)]
                       + [pltpu.VMEM(w.shape, BF16) for w in big_weights],
        compiler_params=pltpu.CompilerParams(dimension_semantics=("arbitrary",),
                                             vmem_limit_bytes=VMEM_LIMIT_BYTES),
        name="layer",
    )(x, *big_weights, *small_params, *sample_ins)


def kernel(x_prompt, x_sample, cache_pool, state_ssm_re, state_ssm_im, norm1_g, w_in, pool_w,
           pool_scale, pool_out, ssm_a_re, ssm_a_im, ssm_log_dt, ssm_b_re, ssm_b_im, ssm_c_re,
           ssm_c_im, ssm_d, ssm_glu, w_out, norm2_g, w_up, w_down, normf_g):
    depth = w_in.shape[0]
    assert depth == 1, "the final norm is fused into the (single) layer kernel"
    batch, seq, d = x_prompt.shape
    dec_batch, dec_seq, _ = x_sample.shape
    assert dec_seq == 1 and batch == SUBLANES and seq % STEPS_PER_TILE == 0
    groups, n_state = ssm_a_re.shape[1:]
    n_states = groups * n_state
    l = 0

    ssm_operands = _ssm_prep(ssm_a_re[l], ssm_a_im[l], ssm_log_dt[l], ssm_b_re[l], ssm_b_im[l],
                             ssm_c_re[l], ssm_c_im[l])
    row = lambda a: a.reshape(1, -1)
    big_weights = (w_in[l], ssm_glu[l], w_out[l], w_up[l], w_down[l])
    small_params = (row(norm1_g[l]), _pool_fold(pool_w[l], pool_scale[l], pool_out[l]), *ssm_operands,
                    row(ssm_d[l]), row(norm2_g[l]), row(normf_g))
    cache_tm = cache_pool[l].transpose(1, 0, 2)
    state_t = lambda s: s.transpose(1, 2, 0).reshape(n_states, dec_batch)
    sample_ins = (x_sample.reshape(dec_batch, d), _pool_hist(cache_tm),
                  state_t(state_ssm_re[l]), state_t(state_ssm_im[l]), cache_tm)

    y_prompt, pool_s_tm, pool_tail, h_fin, y_s, re_s_t, im_s_t = _layer_call(
        x_prompt, big_weights, small_params, sample_ins)

    pool_width = pool_tail.shape[-1]
    pool_p = pool_tail.reshape(POOL_BUF, batch, pool_width).transpose(1, 0, 2)
    h_fin = h_fin.reshape(batch, n_states // SLAB_STATES, 2, SLAB_STATES)
    re_p = h_fin[:, :, 0].reshape(batch, groups, n_state)
    im_p = h_fin[:, :, 1].reshape(batch, groups, n_state)
    pool_s = pool_s_tm.transpose(1, 0, 2)
    state_back = lambda s: s.reshape(groups, n_state, dec_batch).transpose(2, 0, 1)

    return (y_prompt, y_s.reshape(dec_batch, dec_seq, d),
            pool_p[None], re_p[None], im_p[None],
            pool_s[None], state_back(re_s_t)[None], state_back(im_s_t)[None])
```

```python
import functools
import math

import jax
import jax.numpy as jnp
from jax import lax
from jax.experimental import pallas as pl
from jax.experimental.pallas import tpu as pltpu

POOL_WINDOWS = (2, 4, 8, 16)
POOL_BUF = max(POOL_WINDOWS) - 1
SSM_GROUP_W = 16
SSM_STATE = 64
RMS_EPS = 1e-6
PAST_LEN = 16384

LANES = 128
SUBLANES = 8
GROUPS_PER_SLAB = LANES // SSM_GROUP_W
SLAB_STATES = GROUPS_PER_SLAB * SSM_STATE
STEPS_PER_TILE = 32
HALO_STEPS = 16
X_SLOTS = 3
FF_CHUNK = 1024
W_CHUNK_COLS = 1024
W_CAST_ROWS = 32
VMEM_LIMIT_BYTES = 58 * 1024 * 1024

F32 = jnp.float32
BF16 = jnp.bfloat16


def _dot(a, b):
    return jnp.dot(a, b, preferred_element_type=F32)


def _sigmoid(x):
    return 1.0 / (1.0 + jnp.exp(-x))


def _rmsnorm(x, g):
    y = x * lax.rsqrt(jnp.mean(x * x, axis=-1, keepdims=True) + RMS_EPS)
    return y * g


def _full_spec(shape):
    return pl.BlockSpec(shape, lambda i: (0,) * len(shape))


def _discretize(a_re, a_im, dt):
    mag = jnp.exp(a_re * dt)
    ang = a_im * dt
    lam_re = mag * jnp.cos(ang)
    lam_im = mag * jnp.sin(ang)
    p = lam_re - 1.0
    q = lam_im
    den = a_re * a_re + a_im * a_im
    coef_re = (p * a_re + q * a_im) / den
    coef_im = (q * a_re - p * a_im) / den
    return lam_re, lam_im, coef_re, coef_im


def _tile_states(x):
    x2 = jnp.concatenate([x, x], axis=-1)
    return jnp.concatenate([x2] * (SLAB_STATES // (2 * SSM_STATE)), axis=-1)


def _scale_state_cols(m, lr, li):
    re, im = m[:, :SLAB_STATES], m[:, SLAB_STATES:]
    return jnp.concatenate([lr * re - li * im, lr * im + li * re], axis=-1)


def _scale_state_cols_ct(ct, lr, li):
    re, m = ct[:, :SLAB_STATES], ct[:, SLAB_STATES:]
    return jnp.concatenate([lr * re + li * m, lr * m - li * re], axis=-1)


def _ssm_prep_kernel(a_re_ref, a_im_ref, log_dt_ref, b_re_ref, b_im_ref, c_re_ref, c_im_ref,
                     lam_ref, lam2_ref, bc_ref, cc_ref, d_ref):
    groups, n = a_re_ref.shape
    slabs = bc_ref.shape[0]
    chans = b_re_ref.shape[0] // groups
    dt = jnp.exp(log_dt_ref[...])
    a_re = a_re_ref[...]
    a_im = a_im_ref[...]

    lr, li, _, _ = _discretize(_tile_states(a_re), _tile_states(a_im), dt)
    row = lax.broadcasted_iota(jnp.int32, (groups, SLAB_STATES), 0)
    col = lax.broadcasted_iota(jnp.int32, (groups, SLAB_STATES), 1)
    own = (row % GROUPS_PER_SLAB) == (col // n)
    lr = jnp.where(own, lr, 0.0).reshape(slabs, GROUPS_PER_SLAB, SLAB_STATES).sum(axis=1)
    li = jnp.where(own, li, 0.0).reshape(slabs, GROUPS_PER_SLAB, SLAB_STATES).sum(axis=1)
    lam_ref[0] = lr
    lam_ref[1] = li
    lam2_ref[0] = lr * lr - li * li
    lam2_ref[1] = 2.0 * lr * li

    rep = lambda a: jnp.broadcast_to(a[:, None, :], (groups, chans, a.shape[-1])).reshape(
        groups * chans, a.shape[-1])
    _, _, coef_re, coef_im = _discretize(rep(a_re), rep(a_im), rep(dt))
    b_re = b_re_ref[...]
    b_im = b_im_ref[...]
    bbar_re = coef_re * b_re - coef_im * b_im
    bbar_im = coef_re * b_im + coef_im * b_re

    r2 = lax.broadcasted_iota(jnp.int32, (LANES, SLAB_STATES), 0)
    c2 = lax.broadcasted_iota(jnp.int32, (LANES, SLAB_STATES), 1)
    diag = (r2 // chans) == (c2 // n)
    block_diag = lambda m: jnp.where(diag, _tile_states(m), 0.0)
    zero = jnp.zeros((LANES, LANES), F32)
    contract_states = (((1,), (1,)), ((), ()))
    for j in range(slabs):
        rows = slice(LANES * j, LANES * (j + 1))
        lrj, lij = lr[j:j + 1, :], li[j:j + 1, :]
        b = jnp.concatenate([block_diag(bbar_re[rows]), block_diag(bbar_im[rows])], axis=-1)
        ct = jnp.concatenate([block_diag(c_re_ref[rows, :]), -block_diag(c_im_ref[rows, :])], axis=-1)
        lam_b = _scale_state_cols(b, lrj, lij)
        c_lam_t = _scale_state_cols_ct(ct, lrj, lij)
        c_lam2_t = _scale_state_cols_ct(c_lam_t, lrj, lij)
        bc_ref[j] = jnp.concatenate([lam_b, b], axis=0).astype(BF16)
        cc_ref[j] = jnp.concatenate([c_lam_t, c_lam2_t], axis=0).T.astype(BF16)
        cb = lax.dot_general(b, ct, contract_states, precision=lax.Precision.HIGHEST,
                             preferred_element_type=F32)
        c_lam_b = lax.dot_general(lam_b, ct, contract_states, precision=lax.Precision.HIGHEST,
                                  preferred_element_type=F32)
        d_ref[j] = jnp.concatenate([jnp.concatenate([cb, c_lam_b], axis=-1),
                                    jnp.concatenate([zero, cb], axis=-1)], axis=0).astype(BF16)


def _pool_hist_kernel(cache_ref, hist_ref):
    for g, w in enumerate(POOL_WINDOWS):
        cols = slice(g * LANES, (g + 1) * LANES)
        acc = cache_ref[POOL_BUF - 1, :, cols]
        for s in range(2, w):
            acc = acc + cache_ref[POOL_BUF - s, :, cols]
        hist_ref[:, cols] = acc


def _pool_project(pooled, w_pool_ref):
    return _dot(jnp.concatenate(pooled, axis=-1).astype(BF16), w_pool_ref[...])


def _pool_fold_kernel(pool_w_ref, pool_scale_ref, pool_out_ref, w_pool_ref):
    for g in range(len(POOL_WINDOWS)):
        rows = slice(g * LANES, (g + 1) * LANES)
        scaled = pool_w_ref[rows, :] * pool_scale_ref[:, rows]
        w_pool_ref[rows, :] = jnp.dot(scaled, pool_out_ref[rows, :], precision=lax.Precision.HIGHEST,
                                      preferred_element_type=F32).astype(BF16)


def _prep_kernel(a_re_ref, a_im_ref, log_dt_ref, b_re_ref, b_im_ref, c_re_ref, c_im_ref,
                 pool_w_ref, pool_scale_ref, pool_out_ref, cache_ref,
                 lam_ref, lam2_ref, bc_ref, cc_ref, d_ref, w_pool_ref, hist_ref):
    _ssm_prep_kernel(a_re_ref, a_im_ref, log_dt_ref, b_re_ref, b_im_ref, c_re_ref, c_im_ref,
                     lam_ref, lam2_ref, bc_ref, cc_ref, d_ref)
    _pool_fold_kernel(pool_w_ref, pool_scale_ref, pool_out_ref, w_pool_ref)
    _pool_hist_kernel(cache_ref, hist_ref)


def _prep(a_re, a_im, log_dt, b_re, b_im, c_re, c_im, pool_w, pool_scale, pool_out, cache_tm):
    g, n = a_re.shape
    c = b_re.shape[-1]
    slabs = g // GROUPS_PER_SLAB
    pool_groups, pool_gw, _ = pool_w.shape
    ins = (a_re, a_im, log_dt.reshape(g, 1),
           b_re.transpose(0, 2, 1).reshape(g * c, n), b_im.transpose(0, 2, 1).reshape(g * c, n),
           c_re.reshape(g * c, n), c_im.reshape(g * c, n),
           pool_w.reshape(pool_groups * pool_gw, pool_gw), pool_scale.reshape(1, -1), pool_out,
           cache_tm)
    outs = (jax.ShapeDtypeStruct((2, slabs, SLAB_STATES), F32),
            jax.ShapeDtypeStruct((2, slabs, SLAB_STATES), F32),
            jax.ShapeDtypeStruct((slabs, 2 * LANES, 2 * SLAB_STATES), BF16),
            jax.ShapeDtypeStruct((slabs, 2 * SLAB_STATES, 2 * LANES), BF16),
            jax.ShapeDtypeStruct((slabs, 2 * LANES, 2 * LANES), BF16),
            jax.ShapeDtypeStruct(pool_out.shape, BF16),
            jax.ShapeDtypeStruct(cache_tm.shape[1:], F32))
    *ssm_operands, w_pool, hist = pl.pallas_call(
        _prep_kernel,
        out_shape=outs,
        grid=(1,),
        in_specs=[_full_spec(x.shape) for x in ins],
        out_specs=[_full_spec(o.shape) for o in outs],
        compiler_params=pltpu.CompilerParams(vmem_limit_bytes=VMEM_LIMIT_BYTES),
        name="prep",
    )(*ins)
    return ssm_operands, w_pool, hist


def _finish_layer(x, h1, branch_pool, ys, w_in_ref, glu_ref, w_out_ref, g2_ref,
                  w_up_ref, w_down_ref):
    d = x.shape[-1]
    z = _dot(ys.astype(BF16), glu_ref[...])
    branch_ssm = z[:, :d] * _sigmoid(z[:, d:])
    gate_off = w_in_ref.shape[1] - 2 * d
    g_pool = _sigmoid(_dot(h1, w_in_ref[:, gate_off:gate_off + d]))
    g_ssm = _sigmoid(_dot(h1, w_in_ref[:, gate_off + d:]))
    merged = g_pool * branch_pool + g_ssm * branch_ssm
    x1 = x + _dot(merged.astype(BF16), w_out_ref[...])
    h2 = _rmsnorm(x1, g2_ref[...]).astype(BF16)
    d_ff = w_up_ref.shape[1]
    acc = None
    for c0 in range(0, d_ff, FF_CHUNK):
        a = jnp.maximum(_dot(h2, w_up_ref[:, c0:c0 + FF_CHUNK]), 0.0)
        part = _dot((a * a).astype(BF16), w_down_ref[c0:c0 + FF_CHUNK, :])
        acc = part if acc is None else acc + part
    return x1 + acc


def _load_weights_as_bf16(pairs, slots, sem):
    rows, cols = slots[0].shape
    ring = len(slots)
    chunks = []
    for w_hbm, w_vm in pairs:
        k, n = w_hbm.shape
        for r0 in range(0, k, rows):
            for c0 in range(0, n, cols):
                chunks.append((w_hbm, w_vm, r0, c0))

    def copy(idx):
        w_hbm, _, r0, c0 = chunks[idx]
        sl = idx % ring
        return pltpu.make_async_copy(w_hbm.at[pl.ds(r0, rows), pl.ds(c0, cols)], slots[sl], sem.at[sl])

    for idx in range(min(ring, len(chunks))):
        copy(idx).start()
    for idx, (_, w_vm, r0, c0) in enumerate(chunks):
        slot = slots[idx % ring]
        copy(idx).wait()

        def cast_rows(g, carry, w_vm=w_vm, r0=r0, c0=c0, slot=slot):
            r = pl.multiple_of(g * W_CAST_ROWS, W_CAST_ROWS)
            w_vm[pl.ds(r0 + r, W_CAST_ROWS), c0:c0 + cols] = slot[pl.ds(r, W_CAST_ROWS), :].astype(BF16)
            return carry

        lax.fori_loop(0, rows // W_CAST_ROWS, cast_rows, 0)
        if idx + ring < len(chunks):
            copy(idx + ring).start()


def _layer_kernel(n_tiles,
                  x_hbm, w_in_hbm, glu_hbm, w_out_hbm, w_up_hbm, w_down_hbm,
                  g1_ref, w_pool_ref, lam_ref, lam2_ref, bc_ref, cc_ref, d_ref, d_skip_ref,
                  g2_ref, gf_ref,
                  xs_ref, hist_ref, h0t_re_ref, h0t_im_ref, cache_hbm,
                  y_hbm, pool_s_hbm, pool_tail_ref, h_fin_ref, ys_out_ref, hst_re_ref, hst_im_ref,
                  xbuf, ybuf, sem_in, sem_out, sem_w, sem_pool, h1buf, x2buf, vbuf, s_buf, h_state, vs_buf,
                  w_in_ref, glu_ref, w_out_ref, w_up_ref, w_down_ref):
    n_x, steps, batch, d_model = xbuf.shape
    rows = steps * batch
    halo = HALO_STEPS * batch
    n_slabs = bc_ref.shape[0]
    pool_width = vbuf.shape[1]

    def in_copy(tile, b):
        sl = lax.rem(tile, n_x)
        return pltpu.make_async_copy(x_hbm.at[b, pl.ds(tile * steps, steps), :],
                                     xbuf.at[sl, :, b, :], sem_in.at[sl, b])

    def out_copy(tile, b):
        sl = lax.rem(tile, 2)
        return pltpu.make_async_copy(ybuf.at[sl, :, b, :],
                                     y_hbm.at[b, pl.ds(tile * steps, steps), :], sem_out.at[sl, b])

    def final_norm_to_ybuf(sl):
        y = _rmsnorm(x2buf[sl], gf_ref[...])
        ybuf[sl] = y.reshape(steps, batch, d_model)

    def cache_shift_copy():
        return pltpu.make_async_copy(cache_hbm.at[pl.ds(1, POOL_BUF - 1)],
                                     pool_s_hbm.at[pl.ds(0, POOL_BUF - 1)], sem_pool.at[0])

    def cache_new_row_copy():
        return pltpu.make_async_copy(vs_buf, pool_s_hbm.at[POOL_BUF - 1], sem_pool.at[1])

    def ssm_cols(j):
        re_cols = slice(2 * SLAB_STATES * j, 2 * SLAB_STATES * j + SLAB_STATES)
        im_cols = slice(2 * SLAB_STATES * j + SLAB_STATES, 2 * SLAB_STATES * (j + 1))
        return slice(SLAB_STATES * j, SLAB_STATES * (j + 1)), re_cols, im_cols

    def prologue():
        for b in range(batch):
            in_copy(0, b).start()
        for b in range(batch):
            in_copy(1, b).start()
        cache_shift_copy().start()
        stage_slots = ([s_buf.at[:, pl.ds(k * W_CHUNK_COLS, W_CHUNK_COLS)]
                        for k in range(s_buf.shape[1] // W_CHUNK_COLS)]
                       + [x2buf.at[k] for k in range(x2buf.shape[0])])
        _load_weights_as_bf16(
            [(w_in_hbm, w_in_ref), (glu_hbm, glu_ref), (w_out_hbm, w_out_ref),
             (w_up_hbm, w_up_ref), (w_down_hbm, w_down_ref)], stage_slots, sem_w)
        vbuf[0:halo, :] = jnp.zeros((halo, pool_width), F32)
        h_state[...] = jnp.zeros(h_state.shape, F32)
        x2buf[1] = jnp.zeros(x2buf.shape[1:], F32)
        for b in range(batch):
            in_copy(0, b).wait()
        h1buf[0] = _rmsnorm(xbuf[0].reshape(rows, d_model), g1_ref[...]).astype(BF16)

    def prompt_step(i, par):
        xslot = lax.rem(i, n_x)
        xslot_next = lax.rem(i + 1, n_x)

        @pl.when(i + 2 < n_tiles)
        def _():
            for b in range(batch):
                in_copy(i + 2, b).start()

        @pl.when(i + 1 < n_tiles)
        def _():
            for b in range(batch):
                in_copy(i + 1, b).wait()

        @pl.when(i >= 3)
        def _():
            for b in range(batch):
                out_copy(i - 3, b).wait()

        def tile_block():
            final_norm_to_ybuf(1 - par)
            h1buf[1 - par] = _rmsnorm(xbuf[xslot_next].reshape(rows, d_model), g1_ref[...]).astype(BF16)

            x = xbuf[xslot].reshape(rows, d_model)
            h1 = h1buf[par]
            v = _dot(h1, w_in_ref[:, 0:pool_width])
            vbuf[halo:halo + rows, :] = v
            u = _dot(h1, w_in_ref[:, pool_width:2 * pool_width])

            row = lax.broadcasted_iota(jnp.int32, (rows, LANES), 0)
            pos = i * steps + lax.shift_right_logical(row, int(math.log2(batch))) + 1
            pooled = []
            for g, w in enumerate(POOL_WINDOWS):
                cols = slice(g * LANES, (g + 1) * LANES)
                cur = vbuf[halo:halo + rows, cols]
                win = cur
                for s in range(1, w):
                    win = win + vbuf[halo - s * batch:halo - s * batch + rows, cols]
                cnt = jnp.minimum(w, pos).astype(F32)
                pooled.append(win / cnt - cur)
            branch_pool = _pool_project(pooled, w_pool_ref)

            pairs = steps // 2
            half = pairs * batch
            ssm_width = u.shape[1]
            u4 = u.reshape(pairs, 2, batch, ssm_width)
            u_even = u4[:, 0].reshape(half, ssm_width)
            u_odd = u4[:, 1].reshape(half, ssm_width)
            u_pair = [jnp.concatenate([u_even[:, LANES * j:LANES * (j + 1)],
                                       u_odd[:, LANES * j:LANES * (j + 1)]], axis=-1).astype(BF16)
                      for j in range(n_slabs)]
            for j in range(n_slabs):
                s_buf[0:half, 2 * SLAB_STATES * j:2 * SLAB_STATES * (j + 1)] = _dot(u_pair[j], bc_ref[j])
            for j in range(n_slabs):
                _, re_cols, im_cols = ssm_cols(j)
                lr2 = jnp.broadcast_to(lam2_ref[0, j:j + 1, :], (batch, SLAB_STATES))
                li2 = jnp.broadcast_to(lam2_ref[1, j:j + 1, :], (batch, SLAB_STATES))
                hr = h_state[:, re_cols]
                hi = h_state[:, im_cols]
                for p in range(pairs):
                    r = slice(p * batch, (p + 1) * batch)
                    sr = s_buf[r, re_cols]
                    si = s_buf[r, im_cols]
                    s_buf[r, re_cols] = hr
                    s_buf[r, im_cols] = hi
                    hr, hi = lr2 * hr - li2 * hi + sr, lr2 * hi + li2 * hr + si
                h_state[:, re_cols] = hr
                h_state[:, im_cols] = hi
            y_pair = [_dot(s_buf[0:half, 2 * SLAB_STATES * j:2 * SLAB_STATES * (j + 1)].astype(BF16),
                           cc_ref[j]) + _dot(u_pair[j], d_ref[j]) for j in range(n_slabs)]
            y_even = jnp.concatenate([y[:, :LANES] for y in y_pair], axis=-1)
            y_odd = jnp.concatenate([y[:, LANES:] for y in y_pair], axis=-1)
            ys = jnp.concatenate([y_even.reshape(pairs, 1, batch, ssm_width),
                                  y_odd.reshape(pairs, 1, batch, ssm_width)], axis=1).reshape(rows, ssm_width)
            ys = ys + d_skip_ref[...] * u

            x2buf[par] = _finish_layer(x, h1, branch_pool, ys, w_in_ref, glu_ref, w_out_ref, g2_ref,
                                       w_up_ref, w_down_ref)

            tail = vbuf[rows:rows + halo, :]
            vbuf[0:halo, :] = tail
            pool_tail_ref[...] = tail[halo - POOL_BUF * batch:, :]
            h_fin_ref[...] = h_state[...]

        tile_block()

        @pl.when(i >= 1)
        def _():
            for b in range(batch):
                out_copy(i - 1, b).start()

    def epilogue():
        last = n_tiles - 1
        for b in range(batch):
            out_copy(last - 2, b).wait()
        final_norm_to_ybuf(last % 2)
        for b in range(batch):
            out_copy(last, b).start()
        for b in range(batch):
            out_copy(last - 1, b).wait()
        for b in range(batch):
            out_copy(last, b).wait()

    def decode_step():
        x = xs_ref[...]
        h1 = _rmsnorm(x, g1_ref[...]).astype(BF16)
        v = _dot(h1, w_in_ref[:, 0:pool_width])
        vs_buf[...] = v
        cache_new_row_copy().start()
        u = _dot(h1, w_in_ref[:, pool_width:2 * pool_width])

        pooled = []
        for g, w in enumerate(POOL_WINDOWS):
            cols = slice(g * LANES, (g + 1) * LANES)
            cur = v[:, cols]
            cnt = float(min(w, PAST_LEN + 1))
            pooled.append((hist_ref[:, cols] + cur) / cnt - cur)
        branch_pool = _pool_project(pooled, w_pool_ref)

        u_bf = u.astype(BF16)
        ys_parts = []
        for j in range(n_slabs):
            st, _, _ = ssm_cols(j)
            u_j = u_bf[:, LANES * j:LANES * (j + 1)]
            bu = _dot(u_j, bc_ref[j, LANES:2 * LANES, :])
            lr = lam_ref[0, j:j + 1, :]
            li = lam_ref[1, j:j + 1, :]
            hr = h0t_re_ref[st, :].T
            hi = h0t_im_ref[st, :].T
            nr = lr * hr - li * hi + bu[:, :SLAB_STATES]
            ni = lr * hi + li * hr + bu[:, SLAB_STATES:]
            hst_re_ref[st, :] = nr.T
            hst_im_ref[st, :] = ni.T
            h_prev = jnp.concatenate([hr, hi], axis=-1).astype(BF16)
            ys_parts.append(_dot(h_prev, cc_ref[j, :, 0:LANES]) + _dot(u_j, d_ref[j, 0:LANES, 0:LANES]))
        ys = jnp.concatenate(ys_parts, axis=-1) + d_skip_ref[...] * u

        x2 = _finish_layer(x, h1, branch_pool, ys, w_in_ref, glu_ref, w_out_ref, g2_ref,
                           w_up_ref, w_down_ref)
        ys_out_ref[...] = _rmsnorm(x2, gf_ref[...])
        cache_shift_copy().wait()
        cache_new_row_copy().wait()

    def tile_pair(k, carry):
        prompt_step(2 * k, 0)
        prompt_step(2 * k + 1, 1)
        return carry

    prologue()
    lax.fori_loop(0, n_tiles // 2, tile_pair, 0)
    epilogue()
    decode_step()


def _const_spec(shape):
    return pl.BlockSpec(shape, lambda i: (0,) * len(shape), pipeline_mode=pl.Buffered(1))


def _layer_call(x, big_weights, small_params, sample_ins):
    batch, seq, d = x.shape
    steps = STEPS_PER_TILE
    rows = steps * batch
    n_tiles = seq // steps
    assert n_tiles >= X_SLOTS and n_tiles % 2 == 0, "input ring of 3 tiles; tiles are walked in pairs"
    xs, hist, h0t_re, _, cache_tm = sample_ins
    dec_rows = xs.shape[0]
    pool_width = hist.shape[-1]
    n_states = h0t_re.shape[0]
    halo = HALO_STEPS * batch
    assert d == W_CHUNK_COLS and all(w.shape[0] % rows == 0 and w.shape[1] % W_CHUNK_COLS == 0
                                     for w in big_weights), "weight staging chunks must tile every weight"
    out_shape = (jax.ShapeDtypeStruct((batch, seq, d), F32),
                 jax.ShapeDtypeStruct(cache_tm.shape, F32),
                 jax.ShapeDtypeStruct((POOL_BUF * batch, pool_width), F32),
                 jax.ShapeDtypeStruct((batch, 2 * n_states), F32),
                 jax.ShapeDtypeStruct((dec_rows, d), F32),
                 jax.ShapeDtypeStruct((n_states, dec_rows), F32),
                 jax.ShapeDtypeStruct((n_states, dec_rows), F32))
    any_spec = pl.BlockSpec(memory_space=pl.ANY)
    return pl.pallas_call(
        functools.partial(_layer_kernel, n_tiles),
        out_shape=out_shape,
        grid=(1,),
        in_specs=([any_spec] * (1 + len(big_weights))
                  + [_const_spec(p.shape) for p in small_params]
                  + [_const_spec(a.shape) for a in sample_ins[:-1]] + [any_spec]),
        out_specs=[any_spec, any_spec] + [_full_spec(o.shape) for o in out_shape[2:]],
        scratch_shapes=[pltpu.VMEM((X_SLOTS, steps, batch, d), F32),
                        pltpu.VMEM((2, steps, batch, d), F32),
                        pltpu.SemaphoreType.DMA((X_SLOTS, batch)),
                        pltpu.SemaphoreType.DMA((2, batch)),
                        pltpu.SemaphoreType.DMA((2 * n_states // W_CHUNK_COLS + 2,)),
                        pltpu.SemaphoreType.DMA((2,)),
                        pltpu.VMEM((2, rows, d), BF16),
                        pltpu.VMEM((2, rows, d), F32),
                        pltpu.VMEM((halo + rows, pool_width), F32),
                        pltpu.VMEM((rows, 2 * n_states), F32),
                        pltpu.VMEM((batch, 2 * n_states), F32),
                        pltpu.VMEM((dec_rows, pool_width), F32)]
                       + [pltpu.VMEM(w.shape, BF16) for w in big_weights],
        compiler_params=pltpu.CompilerParams(dimension_semantics=("arbitrary",),
                                             vmem_limit_bytes=VMEM_LIMIT_BYTES),
        name="layer",
    )(x, *big_weights, *small_params, *sample_ins)


def kernel(x_prompt, x_sample, cache_pool, state_ssm_re, state_ssm_im, norm1_g, w_in, pool_w,
           pool_scale, pool_out, ssm_a_re, ssm_a_im, ssm_log_dt, ssm_b_re, ssm_b_im, ssm_c_re,
           ssm_c_im, ssm_d, ssm_glu, w_out, norm2_g, w_up, w_down, normf_g):
    depth = w_in.shape[0]
    assert depth == 1, "the final norm is fused into the (single) layer kernel"
    batch, seq, d = x_prompt.shape
    dec_batch, dec_seq, _ = x_sample.shape
    assert dec_seq == 1 and batch == SUBLANES and seq % STEPS_PER_TILE == 0
    groups, n_state = ssm_a_re.shape[1:]
    n_states = groups * n_state
    l = 0

    cache_tm = cache_pool[l].transpose(1, 0, 2)
    state_t = lambda s: s.transpose(1, 2, 0).reshape(n_states, dec_batch)
    ssm_operands, w_pool, hist = _prep(ssm_a_re[l], ssm_a_im[l], ssm_log_dt[l], ssm_b_re[l],
                                       ssm_b_im[l], ssm_c_re[l], ssm_c_im[l],
                                       pool_w[l], pool_scale[l], pool_out[l], cache_tm)
    row = lambda a: a.reshape(1, -1)
    big_weights = (w_in[l], ssm_glu[l], w_out[l], w_up[l], w_down[l])
    small_params = (row(norm1_g[l]), w_pool, *ssm_operands,
                    row(ssm_d[l]), row(norm2_g[l]), row(normf_g))
    sample_ins = (x_sample.reshape(dec_batch, d), hist,
                  state_t(state_ssm_re[l]), state_t(state_ssm_im[l]), cache_tm)

    y_prompt, pool_s_tm, pool_tail, h_fin, y_s, re_s_t, im_s_t = _layer_call(
        x_prompt, big_weights, small_params, sample_ins)

    pool_width = pool_tail.shape[-1]
    pool_p = pool_tail.reshape(POOL_BUF, batch, pool_width).transpose(1, 0, 2)
    h_fin = h_fin.reshape(batch, n_states // SLAB_STATES, 2, SLAB_STATES)
    re_p = h_fin[:, :, 0].reshape(batch, groups, n_state)
    im_p = h_fin[:, :, 1].reshape(batch, groups, n_state)
    pool_s = pool_s_tm.transpose(1, 0, 2)
    state_back = lambda s: s.reshape(groups, n_state, dec_batch).transpose(2, 0, 1)

    return (y_prompt, y_s.reshape(dec_batch, dec_seq, d),
            pool_p[None], re_p[None], im_p[None],
            pool_s[None], state_back(re_s_t)[None], state_back(im_s_t)[None])
```

```python
import functools
import math

import jax
import jax.numpy as jnp
from jax import lax
from jax.experimental import pallas as pl
from jax.experimental.pallas import tpu as pltpu

POOL_WINDOWS = (2, 4, 8, 16)
POOL_BUF = max(POOL_WINDOWS) - 1
SSM_GROUP_W = 16
SSM_STATE = 64
RMS_EPS = 1e-6
PAST_LEN = 16384

LANES = 128
SUBLANES = 8
GROUPS_PER_SLAB = LANES // SSM_GROUP_W
SLAB_STATES = GROUPS_PER_SLAB * SSM_STATE
STEPS_PER_TILE = 32
HALO_STEPS = 16
X_SLOTS = 3
FF_CHUNK = 1024
W_CHUNK_COLS = 1024
W_CAST_ROWS = 32
VMEM_LIMIT_BYTES = 58 * 1024 * 1024

F32 = jnp.float32
BF16 = jnp.bfloat16


def _dot(a, b):
    return jnp.dot(a, b, preferred_element_type=F32)


def _sigmoid(x):
    return 1.0 / (1.0 + jnp.exp(-x))


def _rmsnorm(x, g):
    y = x * lax.rsqrt(jnp.mean(x * x, axis=-1, keepdims=True) + RMS_EPS)
    return y * g


def _full_spec(shape):
    return pl.BlockSpec(shape, lambda i: (0,) * len(shape))


def _discretize(a_re, a_im, dt):
    mag = jnp.exp(a_re * dt)
    ang = a_im * dt
    lam_re = mag * jnp.cos(ang)
    lam_im = mag * jnp.sin(ang)
    p = lam_re - 1.0
    q = lam_im
    den = a_re * a_re + a_im * a_im
    coef_re = (p * a_re + q * a_im) / den
    coef_im = (q * a_re - p * a_im) / den
    return lam_re, lam_im, coef_re, coef_im


def _tile_states(x):
    x2 = jnp.concatenate([x, x], axis=-1)
    return jnp.concatenate([x2] * (SLAB_STATES // (2 * SSM_STATE)), axis=-1)


def _scale_state_cols(m, lr, li):
    re, im = m[:, :SLAB_STATES], m[:, SLAB_STATES:]
    return jnp.concatenate([lr * re - li * im, lr * im + li * re], axis=-1)


def _scale_state_cols_ct(ct, lr, li):
    re, m = ct[:, :SLAB_STATES], ct[:, SLAB_STATES:]
    return jnp.concatenate([lr * re + li * m, lr * m - li * re], axis=-1)


def _ssm_prep_kernel(a_re_ref, a_im_ref, log_dt_ref, b_re_ref, b_im_ref, c_re_ref, c_im_ref,
                     lam_ref, lam2_ref, bc_ref, cc_ref, d_ref):
    groups, n = a_re_ref.shape
    slabs = bc_ref.shape[0]
    chans = b_re_ref.shape[0] // groups
    on_diag = (lax.broadcasted_iota(jnp.int32, (groups, groups), 0)
               == lax.broadcasted_iota(jnp.int32, (groups, groups), 1))
    log_dt = jnp.sum(jnp.where(on_diag, log_dt_ref[...], 0.0), axis=1, keepdims=True)
    dt = jnp.exp(log_dt)
    a_re = a_re_ref[...]
    a_im = a_im_ref[...]

    lr, li, _, _ = _discretize(_tile_states(a_re), _tile_states(a_im), dt)
    row = lax.broadcasted_iota(jnp.int32, (groups, SLAB_STATES), 0)
    col = lax.broadcasted_iota(jnp.int32, (groups, SLAB_STATES), 1)
    own = (row % GROUPS_PER_SLAB) == (col // n)
    lr = jnp.where(own, lr, 0.0).reshape(slabs, GROUPS_PER_SLAB, SLAB_STATES).sum(axis=1)
    li = jnp.where(own, li, 0.0).reshape(slabs, GROUPS_PER_SLAB, SLAB_STATES).sum(axis=1)
    lam_ref[0] = lr
    lam_ref[1] = li
    lam2_ref[0] = lr * lr - li * li
    lam2_ref[1] = 2.0 * lr * li

    rep = lambda a: jnp.broadcast_to(a[:, None, :], (groups, chans, a.shape[-1])).reshape(
        groups * chans, a.shape[-1])
    _, _, coef_re, coef_im = _discretize(rep(a_re), rep(a_im), rep(dt))
    b_re = b_re_ref[...]
    b_im = b_im_ref[...]
    bbar_re = coef_re * b_re - coef_im * b_im
    bbar_im = coef_re * b_im + coef_im * b_re

    r2 = lax.broadcasted_iota(jnp.int32, (LANES, SLAB_STATES), 0)
    c2 = lax.broadcasted_iota(jnp.int32, (LANES, SLAB_STATES), 1)
    diag = (r2 // chans) == (c2 // n)
    block_diag = lambda m: jnp.where(diag, _tile_states(m), 0.0)
    zero = jnp.zeros((LANES, LANES), F32)
    contract_states = (((1,), (1,)), ((), ()))
    for j in range(slabs):
        rows = slice(LANES * j, LANES * (j + 1))
        lrj, lij = lr[j:j + 1, :], li[j:j + 1, :]
        b = jnp.concatenate([block_diag(bbar_re[rows]), block_diag(bbar_im[rows])], axis=-1)
        ct = jnp.concatenate([block_diag(c_re_ref[rows, :]), -block_diag(c_im_ref[rows, :])], axis=-1)
        lam_b = _scale_state_cols(b, lrj, lij)
        c_lam_t = _scale_state_cols_ct(ct, lrj, lij)
        c_lam2_t = _scale_state_cols_ct(c_lam_t, lrj, lij)
        bc_ref[j] = jnp.concatenate([lam_b, b], axis=0).astype(BF16)
        cc_ref[j] = jnp.concatenate([c_lam_t, c_lam2_t], axis=0).T.astype(BF16)
        cb = lax.dot_general(b, ct, contract_states, precision=lax.Precision.HIGHEST,
                             preferred_element_type=F32)
        c_lam_b = lax.dot_general(lam_b, ct, contract_states, precision=lax.Precision.HIGHEST,
                                  preferred_element_type=F32)
        d_ref[j] = jnp.concatenate([jnp.concatenate([cb, c_lam_b], axis=-1),
                                    jnp.concatenate([zero, cb], axis=-1)], axis=0).astype(BF16)


def _pool_hist_kernel(cache_ref, hist_ref):
    for g, w in enumerate(POOL_WINDOWS):
        cols = slice(g * LANES, (g + 1) * LANES)
        acc = cache_ref[POOL_BUF - 1, :, cols]
        for s in range(2, w):
            acc = acc + cache_ref[POOL_BUF - s, :, cols]
        hist_ref[:, cols] = acc


def _pool_project(pooled, w_pool_ref):
    return _dot(jnp.concatenate(pooled, axis=-1).astype(BF16), w_pool_ref[...])


def _pool_fold_kernel(pool_w_ref, pool_scale_ref, pool_out_ref, w_pool_ref):
    for g in range(len(POOL_WINDOWS)):
        rows = slice(g * LANES, (g + 1) * LANES)
        scaled = pool_w_ref[rows, :] * pool_scale_ref[:, rows]
        w_pool_ref[rows, :] = jnp.dot(scaled, pool_out_ref[rows, :], precision=lax.Precision.HIGHEST,
                                      preferred_element_type=F32).astype(BF16)


def _prep_kernel(a_re_ref, a_im_ref, log_dt_ref, b_re_ref, b_im_ref, c_re_ref, c_im_ref,
                 pool_w_ref, pool_scale_ref, pool_out_ref, cache_ref,
                 lam_ref, lam2_ref, bc_ref, cc_ref, d_ref, w_pool_ref, hist_ref):
    _ssm_prep_kernel(a_re_ref, a_im_ref, log_dt_ref, b_re_ref, b_im_ref, c_re_ref, c_im_ref,
                     lam_ref, lam2_ref, bc_ref, cc_ref, d_ref)
    _pool_fold_kernel(pool_w_ref, pool_scale_ref, pool_out_ref, w_pool_ref)
    _pool_hist_kernel(cache_ref, hist_ref)


def _prep(a_re, a_im, log_dt, b_re, b_im, c_re, c_im, pool_w, pool_scale, pool_out, cache_tm):
    g, n = a_re.shape
    c = b_re.shape[-1]
    slabs = g // GROUPS_PER_SLAB
    pool_groups, pool_gw, _ = pool_w.shape
    ins = (a_re, a_im, log_dt.reshape(1, g),
           b_re.transpose(0, 2, 1).reshape(g * c, n), b_im.transpose(0, 2, 1).reshape(g * c, n),
           c_re.reshape(g * c, n), c_im.reshape(g * c, n),
           pool_w.reshape(pool_groups * pool_gw, pool_gw), pool_scale.reshape(1, -1), pool_out,
           cache_tm)
    outs = (jax.ShapeDtypeStruct((2, slabs, SLAB_STATES), F32),
            jax.ShapeDtypeStruct((2, slabs, SLAB_STATES), F32),
            jax.ShapeDtypeStruct((slabs, 2 * LANES, 2 * SLAB_STATES), BF16),
            jax.ShapeDtypeStruct((slabs, 2 * SLAB_STATES, 2 * LANES), BF16),
            jax.ShapeDtypeStruct((slabs, 2 * LANES, 2 * LANES), BF16),
            jax.ShapeDtypeStruct(pool_out.shape, BF16),
            jax.ShapeDtypeStruct(cache_tm.shape[1:], F32))
    *ssm_operands, w_pool, hist = pl.pallas_call(
        _prep_kernel,
        out_shape=outs,
        grid=(1,),
        in_specs=[_full_spec(x.shape) for x in ins],
        out_specs=[_full_spec(o.shape) for o in outs],
        compiler_params=pltpu.CompilerParams(vmem_limit_bytes=VMEM_LIMIT_BYTES),
        name="prep",
    )(*ins)
    return ssm_operands, w_pool, hist


def _finish_layer(x, h1, branch_pool, ys, w_in_ref, glu_ref, w_out_ref, g2_ref,
                  w_up_ref, w_down_ref):
    d = x.shape[-1]
    z = _dot(ys.astype(BF16), glu_ref[...])
    branch_ssm = z[:, :d] * _sigmoid(z[:, d:])
    gate_off = w_in_ref.shape[1] - 2 * d
    g_pool = _sigmoid(_dot(h1, w_in_ref[:, gate_off:gate_off + d]))
    g_ssm = _sigmoid(_dot(h1, w_in_ref[:, gate_off + d:]))
    merged = g_pool * branch_pool + g_ssm * branch_ssm
    x1 = x + _dot(merged.astype(BF16), w_out_ref[...])
    h2 = _rmsnorm(x1, g2_ref[...]).astype(BF16)
    d_ff = w_up_ref.shape[1]
    acc = None
    for c0 in range(0, d_ff, FF_CHUNK):
        a = jnp.maximum(_dot(h2, w_up_ref[:, c0:c0 + FF_CHUNK]), 0.0)
        part = _dot((a * a).astype(BF16), w_down_ref[c0:c0 + FF_CHUNK, :])
        acc = part if acc is None else acc + part
    return x1 + acc


def _load_weights_as_bf16(pairs, slots, sem):
    rows, cols = slots[0].shape
    ring = len(slots)
    chunks = []
    for w_hbm, w_vm in pairs:
        k, n = w_hbm.shape
        for r0 in range(0, k, rows):
            for c0 in range(0, n, cols):
                chunks.append((w_hbm, w_vm, r0, c0))

    def copy(idx):
        w_hbm, _, r0, c0 = chunks[idx]
        sl = idx % ring
        return pltpu.make_async_copy(w_hbm.at[pl.ds(r0, rows), pl.ds(c0, cols)], slots[sl], sem.at[sl])

    for idx in range(min(ring, len(chunks))):
        copy(idx).start(priority=idx % 2)
    for idx, (_, w_vm, r0, c0) in enumerate(chunks):
        slot = slots[idx % ring]
        copy(idx).wait()

        def cast_rows(g, carry, w_vm=w_vm, r0=r0, c0=c0, slot=slot):
            r = pl.multiple_of(g * W_CAST_ROWS, W_CAST_ROWS)
            w_vm[pl.ds(r0 + r, W_CAST_ROWS), c0:c0 + cols] = slot[pl.ds(r, W_CAST_ROWS), :].astype(BF16)
            return carry

        lax.fori_loop(0, rows // W_CAST_ROWS, cast_rows, 0)
        if idx + ring < len(chunks):
            copy(idx + ring).start(priority=(idx + ring) % 2)


def _layer_kernel(n_tiles,
                  x_hbm, w_in_hbm, glu_hbm, w_out_hbm, w_up_hbm, w_down_hbm,
                  g1_ref, w_pool_ref, lam_ref, lam2_ref, bc_ref, cc_ref, d_ref, d_skip_ref,
                  g2_ref, gf_ref,
                  xs_hbm, hist_ref, h0t_re_ref, h0t_im_ref, cache_hbm,
                  y_hbm, pool_s_hbm, ys_hbm, pool_tail_ref, h_fin_ref, hst_re_ref, hst_im_ref,
                  xbuf, ybuf, sem_in, sem_out, sem_w, sem_pool, sem_dec, h1buf, x2buf, vbuf, s_buf, h_state,
                  vs_buf, xs_buf, ys_buf,
                  w_in_ref, glu_ref, w_out_ref, w_up_ref, w_down_ref):
    n_x, steps, batch, d_model = xbuf.shape
    rows = steps * batch
    halo = HALO_STEPS * batch
    n_slabs = bc_ref.shape[0]
    pool_width = vbuf.shape[1]

    def in_copy(tile, b):
        sl = lax.rem(tile, n_x)
        return pltpu.make_async_copy(x_hbm.at[b, pl.ds(tile * steps, steps), :],
                                     xbuf.at[sl, :, b, :], sem_in.at[sl, b])

    def out_copy(tile, b):
        sl = lax.rem(tile, 2)
        return pltpu.make_async_copy(ybuf.at[sl, :, b, :],
                                     y_hbm.at[b, pl.ds(tile * steps, steps), :], sem_out.at[sl, b])

    def final_norm_to_ybuf(sl):
        y = _rmsnorm(x2buf[sl], gf_ref[...])
        ybuf[sl] = y.reshape(steps, batch, d_model)

    def cache_shift_copy():
        return pltpu.make_async_copy(cache_hbm.at[pl.ds(1, POOL_BUF - 1)],
                                     pool_s_hbm.at[pl.ds(0, POOL_BUF - 1)], sem_pool.at[0])

    def cache_new_row_copy():
        return pltpu.make_async_copy(vs_buf, pool_s_hbm.at[POOL_BUF - 1], sem_pool.at[1])

    def xs_copy():
        return pltpu.make_async_copy(xs_hbm.at[:, 0, :], xs_buf, sem_dec.at[0])

    def ys_copy():
        return pltpu.make_async_copy(ys_buf, ys_hbm.at[:, 0, :], sem_dec.at[1])

    def ssm_cols(j):
        re_cols = slice(2 * SLAB_STATES * j, 2 * SLAB_STATES * j + SLAB_STATES)
        im_cols = slice(2 * SLAB_STATES * j + SLAB_STATES, 2 * SLAB_STATES * (j + 1))
        return slice(SLAB_STATES * j, SLAB_STATES * (j + 1)), re_cols, im_cols

    def prologue():
        for b in range(batch):
            in_copy(0, b).start()
        for b in range(batch):
            in_copy(1, b).start()
        cache_shift_copy().start()
        xs_copy().start()
        stage_slots = ([s_buf.at[:, pl.ds(k * W_CHUNK_COLS, W_CHUNK_COLS)]
                        for k in range(s_buf.shape[1] // W_CHUNK_COLS)]
                       + [x2buf.at[k] for k in range(x2buf.shape[0])])
        _load_weights_as_bf16(
            [(w_in_hbm, w_in_ref), (glu_hbm, glu_ref), (w_out_hbm, w_out_ref),
             (w_up_hbm, w_up_ref), (w_down_hbm, w_down_ref)], stage_slots, sem_w)
        vbuf[0:halo, :] = jnp.zeros((halo, pool_width), F32)
        h_state[...] = jnp.zeros(h_state.shape, F32)
        x2buf[1] = jnp.zeros(x2buf.shape[1:], F32)
        for b in range(batch):
            in_copy(0, b).wait()
        h1buf[0] = _rmsnorm(xbuf[0].reshape(rows, d_model), g1_ref[...]).astype(BF16)

    def prompt_step(i, par):
        xslot = lax.rem(i, n_x)
        xslot_next = lax.rem(i + 1, n_x)

        @pl.when(i + 2 < n_tiles)
        def _():
            for b in range(batch):
                in_copy(i + 2, b).start()

        @pl.when(i + 1 < n_tiles)
        def _():
            for b in range(batch):
                in_copy(i + 1, b).wait()

        @pl.when(i >= 3)
        def _():
            for b in range(batch):
                out_copy(i - 3, b).wait()

        def tile_block():
            final_norm_to_ybuf(1 - par)
            h1buf[1 - par] = _rmsnorm(xbuf[xslot_next].reshape(rows, d_model), g1_ref[...]).astype(BF16)

            x = xbuf[xslot].reshape(rows, d_model)
            h1 = h1buf[par]
            v = _dot(h1, w_in_ref[:, 0:pool_width])
            vbuf[halo:halo + rows, :] = v
            u = _dot(h1, w_in_ref[:, pool_width:2 * pool_width])

            row = lax.broadcasted_iota(jnp.int32, (rows, LANES), 0)
            pos = i * steps + lax.shift_right_logical(row, int(math.log2(batch))) + 1
            pooled = []
            for g, w in enumerate(POOL_WINDOWS):
                cols = slice(g * LANES, (g + 1) * LANES)
                cur = vbuf[halo:halo + rows, cols]
                win = cur
                for s in range(1, w):
                    win = win + vbuf[halo - s * batch:halo - s * batch + rows, cols]
                cnt = jnp.minimum(w, pos).astype(F32)
                pooled.append(win / cnt - cur)
            branch_pool = _pool_project(pooled, w_pool_ref)

            pairs = steps // 2
            half = pairs * batch
            ssm_width = u.shape[1]
            u4 = u.reshape(pairs, 2, batch, ssm_width)
            u_even = u4[:, 0].reshape(half, ssm_width)
            u_odd = u4[:, 1].reshape(half, ssm_width)
            u_pair = [jnp.concatenate([u_even[:, LANES * j:LANES * (j + 1)],
                                       u_odd[:, LANES * j:LANES * (j + 1)]], axis=-1).astype(BF16)
                      for j in range(n_slabs)]
            for j in range(n_slabs):
                s_buf[0:half, 2 * SLAB_STATES * j:2 * SLAB_STATES * (j + 1)] = _dot(u_pair[j], bc_ref[j])
            for j in range(n_slabs):
                _, re_cols, im_cols = ssm_cols(j)
                lr2 = jnp.broadcast_to(lam2_ref[0, j:j + 1, :], (batch, SLAB_STATES))
                li2 = jnp.broadcast_to(lam2_ref[1, j:j + 1, :], (batch, SLAB_STATES))
                hr = h_state[:, re_cols]
                hi = h_state[:, im_cols]
                for p in range(pairs):
                    r = slice(p * batch, (p + 1) * batch)
                    sr = s_buf[r, re_cols]
                    si = s_buf[r, im_cols]
                    s_buf[r, re_cols] = hr
                    s_buf[r, im_cols] = hi
                    hr, hi = lr2 * hr - li2 * hi + sr, lr2 * hi + li2 * hr + si
                h_state[:, re_cols] = hr
                h_state[:, im_cols] = hi
            y_pair = [_dot(s_buf[0:half, 2 * SLAB_STATES * j:2 * SLAB_STATES * (j + 1)].astype(BF16),
                           cc_ref[j]) + _dot(u_pair[j], d_ref[j]) for j in range(n_slabs)]
            y_even = jnp.concatenate([y[:, :LANES] for y in y_pair], axis=-1)
            y_odd = jnp.concatenate([y[:, LANES:] for y in y_pair], axis=-1)
            ys = jnp.concatenate([y_even.reshape(pairs, 1, batch, ssm_width),
                                  y_odd.reshape(pairs, 1, batch, ssm_width)], axis=1).reshape(rows, ssm_width)
            ys = ys + d_skip_ref[...] * u

            x2buf[par] = _finish_layer(x, h1, branch_pool, ys, w_in_ref, glu_ref, w_out_ref, g2_ref,
                                       w_up_ref, w_down_ref)

            tail = vbuf[rows:rows + halo, :]
            vbuf[0:halo, :] = tail
            pool_tail_ref[...] = tail[halo - POOL_BUF * batch:, :]
            h_fin_ref[...] = h_state[...]

        tile_block()

        @pl.when(i >= 1)
        def _():
            for b in range(batch):
                out_copy(i - 1, b).start()

    def epilogue():
        last = n_tiles - 1
        for b in range(batch):
            out_copy(last - 2, b).wait()
        final_norm_to_ybuf(last % 2)
        for b in range(batch):
            out_copy(last, b).start()
        for b in range(batch):
            out_copy(last - 1, b).wait()
        for b in range(batch):
            out_copy(last, b).wait()

    def decode_step():
        xs_copy().wait()
        x = xs_buf[...]
        h1 = _rmsnorm(x, g1_ref[...]).astype(BF16)
        v = _dot(h1, w_in_ref[:, 0:pool_width])
        vs_buf[...] = v
        cache_new_row_copy().start()
        u = _dot(h1, w_in_ref[:, pool_width:2 * pool_width])

        pooled = []
        for g, w in enumerate(POOL_WINDOWS):
            cols = slice(g * LANES, (g + 1) * LANES)
            cur = v[:, cols]
            cnt = float(min(w, PAST_LEN + 1))
            pooled.append((hist_ref[:, cols] + cur) / cnt - cur)
        branch_pool = _pool_project(pooled, w_pool_ref)

        u_bf = u.astype(BF16)
        ys_parts = []
        for j in range(n_slabs):
            st, _, _ = ssm_cols(j)
            u_j = u_bf[:, LANES * j:LANES * (j + 1)]
            bu = _dot(u_j, bc_ref[j, LANES:2 * LANES, :])
            lr = lam_ref[0, j:j + 1, :]
            li = lam_ref[1, j:j + 1, :]
            hr = h0t_re_ref[st, :].T
            hi = h0t_im_ref[st, :].T
            nr = lr * hr - li * hi + bu[:, :SLAB_STATES]
            ni = lr * hi + li * hr + bu[:, SLAB_STATES:]
            hst_re_ref[st, :] = nr.T
            hst_im_ref[st, :] = ni.T
            h_prev = jnp.concatenate([hr, hi], axis=-1).astype(BF16)
            ys_parts.append(_dot(h_prev, cc_ref[j, :, 0:LANES]) + _dot(u_j, d_ref[j, 0:LANES, 0:LANES]))
        ys = jnp.concatenate(ys_parts, axis=-1) + d_skip_ref[...] * u

        x2 = _finish_layer(x, h1, branch_pool, ys, w_in_ref, glu_ref, w_out_ref, g2_ref,
                           w_up_ref, w_down_ref)
        ys_buf[...] = _rmsnorm(x2, gf_ref[...])
        ys_copy().start()
        cache_shift_copy().wait()
        cache_new_row_copy().wait()
        ys_copy().wait()

    def tile_pair(k, carry):
        prompt_step(2 * k, 0)
        prompt_step(2 * k + 1, 1)
        return carry

    prologue()
    lax.fori_loop(0, n_tiles // 2, tile_pair, 0)
    epilogue()
    decode_step()


def _const_spec(shape):
    return pl.BlockSpec(shape, lambda i: (0,) * len(shape), pipeline_mode=pl.Buffered(1))


def _layer_call(x, big_weights, small_params, sample_ins):
    batch, seq, d = x.shape
    steps = STEPS_PER_TILE
    rows = steps * batch
    n_tiles = seq // steps
    assert n_tiles >= X_SLOTS and n_tiles % 2 == 0, "input ring of 3 tiles; tiles are walked in pairs"
    xs, hist, h0t_re, _, cache_tm = sample_ins
    dec_rows = xs.shape[0]
    pool_width = hist.shape[-1]
    n_states = h0t_re.shape[0]
    halo = HALO_STEPS * batch
    assert d == W_CHUNK_COLS and all(w.shape[0] % rows == 0 and w.shape[1] % W_CHUNK_COLS == 0
                                     for w in big_weights), "weight staging chunks must tile every weight"
    out_shape = (jax.ShapeDtypeStruct((batch, seq, d), F32),
                 jax.ShapeDtypeStruct(cache_tm.shape, F32),
                 jax.ShapeDtypeStruct(xs.shape, F32),
                 jax.ShapeDtypeStruct((POOL_BUF * batch, pool_width), F32),
                 jax.ShapeDtypeStruct((batch, 2 * n_states), F32),
                 jax.ShapeDtypeStruct((n_states, dec_rows), F32),
                 jax.ShapeDtypeStruct((n_states, dec_rows), F32))
    any_spec = pl.BlockSpec(memory_space=pl.ANY)
    return pl.pallas_call(
        functools.partial(_layer_kernel, n_tiles),
        out_shape=out_shape,
        grid=(1,),
        in_specs=([any_spec] * (1 + len(big_weights))
                  + [_const_spec(p.shape) for p in small_params]
                  + [any_spec] + [_const_spec(a.shape) for a in sample_ins[1:-1]] + [any_spec]),
        out_specs=[any_spec] * 3 + [_full_spec(o.shape) for o in out_shape[3:]],
        scratch_shapes=[pltpu.VMEM((X_SLOTS, steps, batch, d), F32),
                        pltpu.VMEM((2, steps, batch, d), F32),
                        pltpu.SemaphoreType.DMA((X_SLOTS, batch)),
                        pltpu.SemaphoreType.DMA((2, batch)),
                        pltpu.SemaphoreType.DMA((2 * n_states // W_CHUNK_COLS + 2,)),
                        pltpu.SemaphoreType.DMA((2,)),
                        pltpu.SemaphoreType.DMA((2,)),
                        pltpu.VMEM((2, rows, d), BF16),
                        pltpu.VMEM((2, rows, d), F32),
                        pltpu.VMEM((halo + rows, pool_width), F32),
                        pltpu.VMEM((rows, 2 * n_states), F32),
                        pltpu.VMEM((batch, 2 * n_states), F32),
                        pltpu.VMEM((dec_rows, pool_width), F32),
                        pltpu.VMEM((dec_rows, d), F32),
                        pltpu.VMEM((dec_rows, d), F32)]
                       + [pltpu.VMEM(w.shape, BF16) for w in big_weights],
        compiler_params=pltpu.CompilerParams(dimension_semantics=("arbitrary",),
                                             vmem_limit_bytes=VMEM_LIMIT_BYTES),
        name="layer",
    )(x, *big_weights, *small_params, *sample_ins)


def kernel(x_prompt, x_sample, cache_pool, state_ssm_re, state_ssm_im, norm1_g, w_in, pool_w,
           pool_scale, pool_out, ssm_a_re, ssm_a_im, ssm_log_dt, ssm_b_re, ssm_b_im, ssm_c_re,
           ssm_c_im, ssm_d, ssm_glu, w_out, norm2_g, w_up, w_down, normf_g):
    depth = w_in.shape[0]
    assert depth == 1, "the final norm is fused into the (single) layer kernel"
    batch, seq, d = x_prompt.shape
    dec_batch, dec_seq, _ = x_sample.shape
    assert dec_seq == 1 and batch == SUBLANES and seq % STEPS_PER_TILE == 0
    groups, n_state = ssm_a_re.shape[1:]
    n_states = groups * n_state
    l = 0

    cache_tm = cache_pool[l].transpose(1, 0, 2)
    state_t = lambda s: s.transpose(1, 2, 0).reshape(n_states, dec_batch)
    ssm_operands, w_pool, hist = _prep(ssm_a_re[l], ssm_a_im[l], ssm_log_dt[l], ssm_b_re[l],
                                       ssm_b_im[l], ssm_c_re[l], ssm_c_im[l],
                                       pool_w[l], pool_scale[l], pool_out[l], cache_tm)
    row = lambda a: a.reshape(1, -1)
    big_weights = (w_in[l], ssm_glu[l], w_out[l], w_up[l], w_down[l])
    small_params = (row(norm1_g[l]), w_pool, *ssm_operands,
                    row(ssm_d[l]), row(norm2_g[l]), row(normf_g))
    sample_ins = (x_sample, hist, state_t(state_ssm_re[l]), state_t(state_ssm_im[l]), cache_tm)

    y_prompt, pool_s_tm, y_s, pool_tail, h_fin, re_s_t, im_s_t = _layer_call(
        x_prompt, big_weights, small_params, sample_ins)

    pool_width = pool_tail.shape[-1]
    pool_p = pool_tail.reshape(POOL_BUF, batch, pool_width).transpose(1, 0, 2)
    h_fin = h_fin.reshape(batch, n_states // SLAB_STATES, 2, SLAB_STATES)
    re_p = h_fin[:, :, 0].reshape(batch, groups, n_state)
    im_p = h_fin[:, :, 1].reshape(batch, groups, n_state)
    pool_s = pool_s_tm.transpose(1, 0, 2)
    state_back = lambda s: s.reshape(groups, n_state, dec_batch).transpose(2, 0, 1)

    return (y_prompt, y_s,
            pool_p[None], re_p[None], im_p[None],
            pool_s[None], state_back(re_s_t)[None], state_back(im_s_t)[None])
```

```python
import functools
import math

import jax
import jax.numpy as jnp
from jax import lax
from jax.experimental import pallas as pl
from jax.experimental.pallas import tpu as pltpu

POOL_WINDOWS = (2, 4, 8, 16)
POOL_BUF = max(POOL_WINDOWS) - 1
SSM_GROUP_W = 16
SSM_STATE = 64
RMS_EPS = 1e-6
PAST_LEN = 16384

LANES = 128
SUBLANES = 8
GROUPS_PER_SLAB = LANES // SSM_GROUP_W
SLAB_STATES = GROUPS_PER_SLAB * SSM_STATE
STEPS_PER_TILE = 32
HALO_STEPS = 16
X_SLOTS = 3
FF_CHUNK = 1024
W_CHUNK_COLS = 1024
W_CAST_ROWS = 32
VMEM_LIMIT_BYTES = 58 * 1024 * 1024

F32 = jnp.float32
BF16 = jnp.bfloat16


def _dot(a, b):
    return jnp.dot(a, b, preferred_element_type=F32)


def _sigmoid(x):
    return 1.0 / (1.0 + jnp.exp(-x))


def _rmsnorm(x, g):
    y = x * lax.rsqrt(jnp.mean(x * x, axis=-1, keepdims=True) + RMS_EPS)
    return y * g


def _full_spec(shape):
    return pl.BlockSpec(shape, lambda i: (0,) * len(shape))


def _discretize(a_re, a_im, dt):
    mag = jnp.exp(a_re * dt)
    ang = a_im * dt
    lam_re = mag * jnp.cos(ang)
    lam_im = mag * jnp.sin(ang)
    p = lam_re - 1.0
    q = lam_im
    den = a_re * a_re + a_im * a_im
    coef_re = (p * a_re + q * a_im) / den
    coef_im = (q * a_re - p * a_im) / den
    return lam_re, lam_im, coef_re, coef_im


def _tile_states(x):
    x2 = jnp.concatenate([x, x], axis=-1)
    return jnp.concatenate([x2] * (SLAB_STATES // (2 * SSM_STATE)), axis=-1)


def _scale_state_cols(m, lr, li):
    re, im = m[:, :SLAB_STATES], m[:, SLAB_STATES:]
    return jnp.concatenate([lr * re - li * im, lr * im + li * re], axis=-1)


def _scale_state_cols_ct(ct, lr, li):
    re, m = ct[:, :SLAB_STATES], ct[:, SLAB_STATES:]
    return jnp.concatenate([lr * re + li * m, lr * m - li * re], axis=-1)


def _ssm_prep_kernel(a_re_ref, a_im_ref, log_dt_ref, b_re_ref, b_im_ref, c_re_ref, c_im_ref,
                     lam_ref, lam2_ref, bc_ref, cc_ref, d_ref):
    groups, n = a_re_ref.shape
    slabs = bc_ref.shape[0]
    chans = b_re_ref.shape[0] // groups
    on_diag = (lax.broadcasted_iota(jnp.int32, (groups, groups), 0)
               == lax.broadcasted_iota(jnp.int32, (groups, groups), 1))
    log_dt = jnp.sum(jnp.where(on_diag, log_dt_ref[...], 0.0), axis=1, keepdims=True)
    dt = jnp.exp(log_dt)
    a_re = a_re_ref[...]
    a_im = a_im_ref[...]

    lr, li, _, _ = _discretize(_tile_states(a_re), _tile_states(a_im), dt)
    row = lax.broadcasted_iota(jnp.int32, (groups, SLAB_STATES), 0)
    col = lax.broadcasted_iota(jnp.int32, (groups, SLAB_STATES), 1)
    own = (row % GROUPS_PER_SLAB) == (col // n)
    lr = jnp.where(own, lr, 0.0).reshape(slabs, GROUPS_PER_SLAB, SLAB_STATES).sum(axis=1)
    li = jnp.where(own, li, 0.0).reshape(slabs, GROUPS_PER_SLAB, SLAB_STATES).sum(axis=1)
    lam_ref[0] = lr
    lam_ref[1] = li
    lam2_ref[0] = lr * lr - li * li
    lam2_ref[1] = 2.0 * lr * li

    rep = lambda a: jnp.broadcast_to(a[:, None, :], (groups, chans, a.shape[-1])).reshape(
        groups * chans, a.shape[-1])
    lam_re_rows, lam_im_rows, coef_re, coef_im = _discretize(rep(a_re), rep(a_im), rep(dt))
    b_re = b_re_ref[...]
    b_im = b_im_ref[...]
    bbar_re = coef_re * b_re - coef_im * b_im
    bbar_im = coef_re * b_im + coef_im * b_re

    r2 = lax.broadcasted_iota(jnp.int32, (LANES, SLAB_STATES), 0)
    c2 = lax.broadcasted_iota(jnp.int32, (LANES, SLAB_STATES), 1)
    diag = (r2 // chans) == (c2 // n)
    block_diag = lambda m: jnp.where(diag, _tile_states(m), 0.0)
    zero = jnp.zeros((LANES, LANES), F32)
    same_group = (lax.broadcasted_iota(jnp.int32, (LANES, LANES), 0) // chans
                  == lax.broadcasted_iota(jnp.int32, (LANES, LANES), 1) // chans)

    def feed_through(m_re, m_im, rows):
        contract_states = (((1,), (1,)), ((), ()))
        full = (lax.dot_general(m_re, c_re_ref[rows, :], contract_states,
                                precision=lax.Precision.HIGHEST, preferred_element_type=F32)
                - lax.dot_general(m_im, c_im_ref[rows, :], contract_states,
                                  precision=lax.Precision.HIGHEST, preferred_element_type=F32))
        return jnp.where(same_group, full, 0.0)

    for j in range(slabs):
        rows = slice(LANES * j, LANES * (j + 1))
        lrj, lij = lr[j:j + 1, :], li[j:j + 1, :]
        b = jnp.concatenate([block_diag(bbar_re[rows]), block_diag(bbar_im[rows])], axis=-1)
        ct = jnp.concatenate([block_diag(c_re_ref[rows, :]), -block_diag(c_im_ref[rows, :])], axis=-1)
        lam_b = _scale_state_cols(b, lrj, lij)
        c_lam_t = _scale_state_cols_ct(ct, lrj, lij)
        c_lam2_t = _scale_state_cols_ct(c_lam_t, lrj, lij)
        bc_ref[j] = jnp.concatenate([lam_b, b], axis=0).astype(BF16)
        cc_ref[j] = jnp.concatenate([c_lam_t, c_lam2_t], axis=0).T.astype(BF16)
        cb = feed_through(bbar_re[rows], bbar_im[rows], rows)
        lam_r, lam_i = lam_re_rows[rows], lam_im_rows[rows]
        c_lam_b = feed_through(lam_r * bbar_re[rows] - lam_i * bbar_im[rows],
                               lam_r * bbar_im[rows] + lam_i * bbar_re[rows], rows)
        d_ref[j] = jnp.concatenate([jnp.concatenate([cb, c_lam_b], axis=-1),
                                    jnp.concatenate([zero, cb], axis=-1)], axis=0).astype(BF16)


def _pool_hist_kernel(cache_ref, hist_ref):
    for g, w in enumerate(POOL_WINDOWS):
        cols = slice(g * LANES, (g + 1) * LANES)
        acc = cache_ref[POOL_BUF - 1, :, cols]
        for s in range(2, w):
            acc = acc + cache_ref[POOL_BUF - s, :, cols]
        hist_ref[:, cols] = acc


def _pool_project(pooled, w_pool_ref):
    return _dot(jnp.concatenate(pooled, axis=-1).astype(BF16), w_pool_ref[...])


def _pool_fold_kernel(pool_w_ref, pool_scale_ref, pool_out_ref, w_pool_ref):
    for g in range(len(POOL_WINDOWS)):
        rows = slice(g * LANES, (g + 1) * LANES)
        scaled = pool_w_ref[rows, :] * pool_scale_ref[:, rows]
        w_pool_ref[rows, :] = jnp.dot(scaled, pool_out_ref[rows, :], precision=lax.Precision.HIGHEST,
                                      preferred_element_type=F32).astype(BF16)


def _prep_kernel(a_re_ref, a_im_ref, log_dt_ref, b_re_ref, b_im_ref, c_re_ref, c_im_ref,
                 pool_w_ref, pool_scale_ref, pool_out_ref, cache_ref,
                 lam_ref, lam2_ref, bc_ref, cc_ref, d_ref, w_pool_ref, hist_ref):
    _ssm_prep_kernel(a_re_ref, a_im_ref, log_dt_ref, b_re_ref, b_im_ref, c_re_ref, c_im_ref,
                     lam_ref, lam2_ref, bc_ref, cc_ref, d_ref)
    _pool_fold_kernel(pool_w_ref, pool_scale_ref, pool_out_ref, w_pool_ref)
    _pool_hist_kernel(cache_ref, hist_ref)


def _prep(a_re, a_im, log_dt, b_re, b_im, c_re, c_im, pool_w, pool_scale, pool_out, cache_tm):
    g, n = a_re.shape
    c = b_re.shape[-1]
    slabs = g // GROUPS_PER_SLAB
    pool_groups, pool_gw, _ = pool_w.shape
    ins = (a_re, a_im, log_dt.reshape(1, g),
           b_re.transpose(0, 2, 1).reshape(g * c, n), b_im.transpose(0, 2, 1).reshape(g * c, n),
           c_re.reshape(g * c, n), c_im.reshape(g * c, n),
           pool_w.reshape(pool_groups * pool_gw, pool_gw), pool_scale.reshape(1, -1), pool_out,
           cache_tm)
    outs = (jax.ShapeDtypeStruct((2, slabs, SLAB_STATES), F32),
            jax.ShapeDtypeStruct((2, slabs, SLAB_STATES), F32),
            jax.ShapeDtypeStruct((slabs, 2 * LANES, 2 * SLAB_STATES), BF16),
            jax.ShapeDtypeStruct((slabs, 2 * SLAB_STATES, 2 * LANES), BF16),
            jax.ShapeDtypeStruct((slabs, 2 * LANES, 2 * LANES), BF16),
            jax.ShapeDtypeStruct(pool_out.shape, BF16),
            jax.ShapeDtypeStruct(cache_tm.shape[1:], F32))
    *ssm_operands, w_pool, hist = pl.pallas_call(
        _prep_kernel,
        out_shape=outs,
        grid=(1,),
        in_specs=[_full_spec(x.shape) for x in ins],
        out_specs=[_full_spec(o.shape) for o in outs],
        compiler_params=pltpu.CompilerParams(vmem_limit_bytes=VMEM_LIMIT_BYTES),
        name="prep",
    )(*ins)
    return ssm_operands, w_pool, hist


def _finish_layer(x, h1, branch_pool, ys, w_in_ref, glu_ref, w_out_ref, g2_ref,
                  w_up_ref, w_down_ref):
    d = x.shape[-1]
    z = _dot(ys.astype(BF16), glu_ref[...])
    branch_ssm = z[:, :d] * _sigmoid(z[:, d:])
    gate_off = w_in_ref.shape[1] - 2 * d
    g_pool = _sigmoid(_dot(h1, w_in_ref[:, gate_off:gate_off + d]))
    g_ssm = _sigmoid(_dot(h1, w_in_ref[:, gate_off + d:]))
    merged = g_pool * branch_pool + g_ssm * branch_ssm
    x1 = x + _dot(merged.astype(BF16), w_out_ref[...])
    h2 = _rmsnorm(x1, g2_ref[...]).astype(BF16)
    d_ff = w_up_ref.shape[1]
    acc = None
    for c0 in range(0, d_ff, FF_CHUNK):
        a = jnp.maximum(_dot(h2, w_up_ref[:, c0:c0 + FF_CHUNK]), 0.0)
        part = _dot((a * a).astype(BF16), w_down_ref[c0:c0 + FF_CHUNK, :])
        acc = part if acc is None else acc + part
    return x1 + acc


def _load_weights_as_bf16(pairs, slots, sem):
    rows, cols = slots[0].shape
    ring = len(slots)
    chunks = []
    for w_hbm, w_vm in pairs:
        k, n = w_hbm.shape
        for r0 in range(0, k, rows):
            for c0 in range(0, n, cols):
                chunks.append((w_hbm, w_vm, r0, c0))

    def copy(idx):
        w_hbm, _, r0, c0 = chunks[idx]
        sl = idx % ring
        return pltpu.make_async_copy(w_hbm.at[pl.ds(r0, rows), pl.ds(c0, cols)], slots[sl], sem.at[sl])

    for idx in range(min(ring, len(chunks))):
        copy(idx).start(priority=idx % 2)
    for idx, (_, w_vm, r0, c0) in enumerate(chunks):
        slot = slots[idx % ring]
        copy(idx).wait()

        def cast_rows(g, carry, w_vm=w_vm, r0=r0, c0=c0, slot=slot):
            r = pl.multiple_of(g * W_CAST_ROWS, W_CAST_ROWS)
            w_vm[pl.ds(r0 + r, W_CAST_ROWS), c0:c0 + cols] = slot[pl.ds(r, W_CAST_ROWS), :].astype(BF16)
            return carry

        lax.fori_loop(0, rows // W_CAST_ROWS, cast_rows, 0)
        if idx + ring < len(chunks):
            copy(idx + ring).start(priority=(idx + ring) % 2)


def _layer_kernel(n_tiles,
                  x_hbm, w_in_hbm, glu_hbm, w_out_hbm, w_up_hbm, w_down_hbm,
                  g1_ref, w_pool_ref, lam_ref, lam2_ref, bc_ref, cc_ref, d_ref, d_skip_ref,
                  g2_ref, gf_ref,
                  xs_hbm, hist_ref, h0t_re_ref, h0t_im_ref, cache_hbm,
                  y_hbm, pool_s_hbm, ys_hbm, pool_tail_ref, h_fin_ref, hst_re_ref, hst_im_ref,
                  xbuf, ybuf, sem_in, sem_out, sem_w, sem_pool, sem_dec, h1buf, x2buf, vbuf, s_buf, h_state,
                  vs_buf, xs_buf, ys_buf,
                  w_in_ref, glu_ref, w_out_ref, w_up_ref, w_down_ref):
    n_x, steps, batch, d_model = xbuf.shape
    rows = steps * batch
    halo = HALO_STEPS * batch
    n_slabs = bc_ref.shape[0]
    pool_width = vbuf.shape[1]

    def in_copy(tile, b):
        sl = lax.rem(tile, n_x)
        return pltpu.make_async_copy(x_hbm.at[b, pl.ds(tile * steps, steps), :],
                                     xbuf.at[sl, :, b, :], sem_in.at[sl, b])

    def out_copy(tile, b):
        sl = lax.rem(tile, 2)
        return pltpu.make_async_copy(ybuf.at[sl, :, b, :],
                                     y_hbm.at[b, pl.ds(tile * steps, steps), :], sem_out.at[sl, b])

    def final_norm_to_ybuf(sl):
        y = _rmsnorm(x2buf[sl], gf_ref[...])
        ybuf[sl] = y.reshape(steps, batch, d_model)

    def cache_shift_copy():
        return pltpu.make_async_copy(cache_hbm.at[pl.ds(1, POOL_BUF - 1)],
                                     pool_s_hbm.at[pl.ds(0, POOL_BUF - 1)], sem_pool.at[0])

    def cache_new_row_copy():
        return pltpu.make_async_copy(vs_buf, pool_s_hbm.at[POOL_BUF - 1], sem_pool.at[1])

    def xs_copy():
        return pltpu.make_async_copy(xs_hbm.at[:, 0, :], xs_buf, sem_dec.at[0])

    def ys_copy():
        return pltpu.make_async_copy(ys_buf, ys_hbm.at[:, 0, :], sem_dec.at[1])

    def ssm_cols(j):
        re_cols = slice(2 * SLAB_STATES * j, 2 * SLAB_STATES * j + SLAB_STATES)
        im_cols = slice(2 * SLAB_STATES * j + SLAB_STATES, 2 * SLAB_STATES * (j + 1))
        return slice(SLAB_STATES * j, SLAB_STATES * (j + 1)), re_cols, im_cols

    def prologue():
        for b in range(batch):
            in_copy(0, b).start()
        for b in range(batch):
            in_copy(1, b).start()
        cache_shift_copy().start()
        xs_copy().start()
        stage_slots = ([s_buf.at[:, pl.ds(k * W_CHUNK_COLS, W_CHUNK_COLS)]
                        for k in range(s_buf.shape[1] // W_CHUNK_COLS)]
                       + [x2buf.at[k] for k in range(x2buf.shape[0])])
        _load_weights_as_bf16(
            [(w_in_hbm, w_in_ref), (glu_hbm, glu_ref), (w_out_hbm, w_out_ref),
             (w_up_hbm, w_up_ref), (w_down_hbm, w_down_ref)], stage_slots, sem_w)
        vbuf[0:halo, :] = jnp.zeros((halo, pool_width), F32)
        h_state[...] = jnp.zeros(h_state.shape, F32)
        x2buf[1] = jnp.zeros(x2buf.shape[1:], F32)
        for b in range(batch):
            in_copy(0, b).wait()
        h1buf[0] = _rmsnorm(xbuf[0].reshape(rows, d_model), g1_ref[...]).astype(BF16)

    def prompt_step(i, par):
        xslot = lax.rem(i, n_x)
        xslot_next = lax.rem(i + 1, n_x)

        @pl.when(i + 2 < n_tiles)
        def _():
            for b in range(batch):
                in_copy(i + 2, b).start()

        @pl.when(i + 1 < n_tiles)
        def _():
            for b in range(batch):
                in_copy(i + 1, b).wait()

        @pl.when(i >= 3)
        def _():
            for b in range(batch):
                out_copy(i - 3, b).wait()

        def tile_block():
            final_norm_to_ybuf(1 - par)
            h1buf[1 - par] = _rmsnorm(xbuf[xslot_next].reshape(rows, d_model), g1_ref[...]).astype(BF16)

            x = xbuf[xslot].reshape(rows, d_model)
            h1 = h1buf[par]
            v = _dot(h1, w_in_ref[:, 0:pool_width])
            vbuf[halo:halo + rows, :] = v
            u = _dot(h1, w_in_ref[:, pool_width:2 * pool_width])

            row = lax.broadcasted_iota(jnp.int32, (rows, LANES), 0)
            pos = i * steps + lax.shift_right_logical(row, int(math.log2(batch))) + 1
            pooled = []
            for g, w in enumerate(POOL_WINDOWS):
                cols = slice(g * LANES, (g + 1) * LANES)
                cur = vbuf[halo:halo + rows, cols]
                win = cur
                for s in range(1, w):
                    win = win + vbuf[halo - s * batch:halo - s * batch + rows, cols]
                cnt = jnp.minimum(w, pos).astype(F32)
                pooled.append(win / cnt - cur)
            branch_pool = _pool_project(pooled, w_pool_ref)

            pairs = steps // 2
            half = pairs * batch
            ssm_width = u.shape[1]
            u4 = u.reshape(pairs, 2, batch, ssm_width)
            u_even = u4[:, 0].reshape(half, ssm_width)
            u_odd = u4[:, 1].reshape(half, ssm_width)
            u_pair = [jnp.concatenate([u_even[:, LANES * j:LANES * (j + 1)],
                                       u_odd[:, LANES * j:LANES * (j + 1)]], axis=-1).astype(BF16)
                      for j in range(n_slabs)]
            for j in range(n_slabs):
                s_buf[0:half, 2 * SLAB_STATES * j:2 * SLAB_STATES * (j + 1)] = _dot(u_pair[j], bc_ref[j])
            for j in range(n_slabs):
                _, re_cols, im_cols = ssm_cols(j)
                lr2 = jnp.broadcast_to(lam2_ref[0, j:j + 1, :], (batch, SLAB_STATES))
                li2 = jnp.broadcast_to(lam2_ref[1, j:j + 1, :], (batch, SLAB_STATES))
                hr = h_state[:, re_cols]
                hi = h_state[:, im_cols]
                for p in range(pairs):
                    r = slice(p * batch, (p + 1) * batch)
                    sr = s_buf[r, re_cols]
                    si = s_buf[r, im_cols]
                    s_buf[r, re_cols] = hr
                    s_buf[r, im_cols] = hi
                    hr, hi = lr2 * hr - li2 * hi + sr, lr2 * hi + li2 * hr + si
                h_state[:, re_cols] = hr
                h_state[:, im_cols] = hi
            y_pair = [_dot(s_buf[0:half, 2 * SLAB_STATES * j:2 * SLAB_STATES * (j + 1)].astype(BF16),
                           cc_ref[j]) + _dot(u_pair[j], d_ref[j]) for j in range(n_slabs)]
            y_even = jnp.concatenate([y[:, :LANES] for y in y_pair], axis=-1)
            y_odd = jnp.concatenate([y[:, LANES:] for y in y_pair], axis=-1)
            ys = jnp.concatenate([y_even.reshape(pairs, 1, batch, ssm_width),
                                  y_odd.reshape(pairs, 1, batch, ssm_width)], axis=1).reshape(rows, ssm_width)
            ys = ys + d_skip_ref[...] * u

            x2buf[par] = _finish_layer(x, h1, branch_pool, ys, w_in_ref, glu_ref, w_out_ref, g2_ref,
                                       w_up_ref, w_down_ref)

            tail = vbuf[rows:rows + halo, :]
            vbuf[0:halo, :] = tail
            pool_tail_ref[...] = tail[halo - POOL_BUF * batch:, :]
            h_fin_ref[...] = h_state[...]

        tile_block()

        @pl.when(i >= 1)
        def _():
            for b in range(batch):
                out_copy(i - 1, b).start()

    def epilogue():
        last = n_tiles - 1
        for b in range(batch):
            out_copy(last - 2, b).wait()
        final_norm_to_ybuf(last % 2)
        for b in range(batch):
            out_copy(last, b).start()
        for b in range(batch):
            out_copy(last - 1, b).wait()
        for b in range(batch):
            out_copy(last, b).wait()

    def decode_step():
        xs_copy().wait()
        x = xs_buf[...]
        h1 = _rmsnorm(x, g1_ref[...]).astype(BF16)
        v = _dot(h1, w_in_ref[:, 0:pool_width])
        vs_buf[...] = v
        cache_new_row_copy().start()
        u = _dot(h1, w_in_ref[:, pool_width:2 * pool_width])

        pooled = []
        for g, w in enumerate(POOL_WINDOWS):
            cols = slice(g * LANES, (g + 1) * LANES)
            cur = v[:, cols]
            cnt = float(min(w, PAST_LEN + 1))
            pooled.append((hist_ref[:, cols] + cur) / cnt - cur)
        branch_pool = _pool_project(pooled, w_pool_ref)

        u_bf = u.astype(BF16)
        ys_parts = []
        for j in range(n_slabs):
            st, _, _ = ssm_cols(j)
            u_j = u_bf[:, LANES * j:LANES * (j + 1)]
            bu = _dot(u_j, bc_ref[j, LANES:2 * LANES, :])
            lr = lam_ref[0, j:j + 1, :]
            li = lam_ref[1, j:j + 1, :]
            hr = h0t_re_ref[st, :].T
            hi = h0t_im_ref[st, :].T
            nr = lr * hr - li * hi + bu[:, :SLAB_STATES]
            ni = lr * hi + li * hr + bu[:, SLAB_STATES:]
            hst_re_ref[st, :] = nr.T
            hst_im_ref[st, :] = ni.T
            h_prev = jnp.concatenate([hr, hi], axis=-1).astype(BF16)
            ys_parts.append(_dot(h_prev, cc_ref[j, :, 0:LANES]) + _dot(u_j, d_ref[j, 0:LANES, 0:LANES]))
        ys = jnp.concatenate(ys_parts, axis=-1) + d_skip_ref[...] * u

        x2 = _finish_layer(x, h1, branch_pool, ys, w_in_ref, glu_ref, w_out_ref, g2_ref,
                           w_up_ref, w_down_ref)
        ys_buf[...] = _rmsnorm(x2, gf_ref[...])
        ys_copy().start()
        cache_shift_copy().wait()
        cache_new_row_copy().wait()
        ys_copy().wait()

    def tile_pair(k, carry):
        prompt_step(2 * k, 0)
        prompt_step(2 * k + 1, 1)
        return carry

    prologue()
    lax.fori_loop(0, n_tiles // 2, tile_pair, 0)
    epilogue()
    decode_step()


def _const_spec(shape):
    return pl.BlockSpec(shape, lambda i: (0,) * len(shape), pipeline_mode=pl.Buffered(1))


def _layer_call(x, big_weights, small_params, sample_ins):
    batch, seq, d = x.shape
    steps = STEPS_PER_TILE
    rows = steps * batch
    n_tiles = seq // steps
    assert n_tiles >= X_SLOTS and n_tiles % 2 == 0, "input ring of 3 tiles; tiles are walked in pairs"
    xs, hist, h0t_re, _, cache_tm = sample_ins
    dec_rows = xs.shape[0]
    pool_width = hist.shape[-1]
    n_states = h0t_re.shape[0]
    halo = HALO_STEPS * batch
    assert d == W_CHUNK_COLS and all(w.shape[0] % rows == 0 and w.shape[1] % W_CHUNK_COLS == 0
                                     for w in big_weights), "weight staging chunks must tile every weight"
    out_shape = (jax.ShapeDtypeStruct((batch, seq, d), F32),
                 jax.ShapeDtypeStruct(cache_tm.shape, F32),
                 jax.ShapeDtypeStruct(xs.shape, F32),
                 jax.ShapeDtypeStruct((POOL_BUF * batch, pool_width), F32),
                 jax.ShapeDtypeStruct((batch, 2 * n_states), F32),
                 jax.ShapeDtypeStruct((n_states, dec_rows), F32),
                 jax.ShapeDtypeStruct((n_states, dec_rows), F32))
    any_spec = pl.BlockSpec(memory_space=pl.ANY)
    return pl.pallas_call(
        functools.partial(_layer_kernel, n_tiles),
        out_shape=out_shape,
        grid=(1,),
        in_specs=([any_spec] * (1 + len(big_weights))
                  + [_const_spec(p.shape) for p in small_params]
                  + [any_spec] + [_const_spec(a.shape) for a in sample_ins[1:-1]] + [any_spec]),
        out_specs=[any_spec] * 3 + [_full_spec(o.shape) for o in out_shape[3:]],
        scratch_shapes=[pltpu.VMEM((X_SLOTS, steps, batch, d), F32),
                        pltpu.VMEM((2, steps, batch, d), F32),
                        pltpu.SemaphoreType.DMA((X_SLOTS, batch)),
                        pltpu.SemaphoreType.DMA((2, batch)),
                        pltpu.SemaphoreType.DMA((2 * n_states // W_CHUNK_COLS + 2,)),
                        pltpu.SemaphoreType.DMA((2,)),
                        pltpu.SemaphoreType.DMA((2,)),
                        pltpu.VMEM((2, rows, d), BF16),
                        pltpu.VMEM((2, rows, d), F32),
                        pltpu.VMEM((halo + rows, pool_width), F32),
                        pltpu.VMEM((rows, 2 * n_states), F32),
                        pltpu.VMEM((batch, 2 * n_states), F32),
                        pltpu.VMEM((dec_rows, pool_width), F32),
                        pltpu.VMEM((dec_rows, d), F32),
                        pltpu.VMEM((dec_rows, d), F32)]
                       + [pltpu.VMEM(w.shape, BF16) for w in big_weights],
        compiler_params=pltpu.CompilerParams(dimension_semantics=("arbitrary",),
                                             vmem_limit_bytes=VMEM_LIMIT_BYTES),
        name="layer",
    )(x, *big_weights, *small_params, *sample_ins)


def kernel(x_prompt, x_sample, cache_pool, state_ssm_re, state_ssm_im, norm1_g, w_in, pool_w,
           pool_scale, pool_out, ssm_a_re, ssm_a_im, ssm_log_dt, ssm_b_re, ssm_b_im, ssm_c_re,
           ssm_c_im, ssm_d, ssm_glu, w_out, norm2_g, w_up, w_down, normf_g):
    depth = w_in.shape[0]
    assert depth == 1, "the final norm is fused into the (single) layer kernel"
    batch, seq, d = x_prompt.shape
    dec_batch, dec_seq, _ = x_sample.shape
    assert dec_seq == 1 and batch == SUBLANES and seq % STEPS_PER_TILE == 0
    groups, n_state = ssm_a_re.shape[1:]
    n_states = groups * n_state
    l = 0

    cache_tm = cache_pool[l].transpose(1, 0, 2)
    state_t = lambda s: s.transpose(1, 2, 0).reshape(n_states, dec_batch)
    ssm_operands, w_pool, hist = _prep(ssm_a_re[l], ssm_a_im[l], ssm_log_dt[l], ssm_b_re[l],
                                       ssm_b_im[l], ssm_c_re[l], ssm_c_im[l],
                                       pool_w[l], pool_scale[l], pool_out[l], cache_tm)
    row = lambda a: a.reshape(1, -1)
    big_weights = (w_in[l], ssm_glu[l], w_out[l], w_up[l], w_down[l])
    small_params = (row(norm1_g[l]), w_pool, *ssm_operands,
                    row(ssm_d[l]), row(norm2_g[l]), row(normf_g))
    sample_ins = (x_sample, hist, state_t(state_ssm_re[l]), state_t(state_ssm_im[l]), cache_tm)

    y_prompt, pool_s_tm, y_s, pool_tail, h_fin, re_s_t, im_s_t = _layer_call(
        x_prompt, big_weights, small_params, sample_ins)

    pool_width = pool_tail.shape[-1]
    pool_p = pool_tail.reshape(POOL_BUF, batch, pool_width).transpose(1, 0, 2)
    h_fin = h_fin.reshape(batch, n_states // SLAB_STATES, 2, SLAB_STATES)
    re_p = h_fin[:, :, 0].reshape(batch, groups, n_state)
    im_p = h_fin[:, :, 1].reshape(batch, groups, n_state)
    pool_s = pool_s_tm.transpose(1, 0, 2)
    state_back = lambda s: s.reshape(groups, n_state, dec_batch).transpose(2, 0, 1)

    return (y_prompt, y_s,
            pool_p[None], re_p[None], im_p[None],
            pool_s[None], state_back(re_s_t)[None], state_back(im_s_t)[None])
```

```python
import functools
import math

import jax
import jax.numpy as jnp
from jax import lax
from jax.experimental import pallas as pl
from jax.experimental.pallas import tpu as pltpu

POOL_WINDOWS = (2, 4, 8, 16)
POOL_BUF = max(POOL_WINDOWS) - 1
SSM_GROUP_W = 16
SSM_STATE = 64
RMS_EPS = 1e-6
PAST_LEN = 16384

LANES = 128
SUBLANES = 8
GROUPS_PER_SLAB = LANES // SSM_GROUP_W
SLAB_STATES = GROUPS_PER_SLAB * SSM_STATE
STEPS_PER_TILE = 32
HALO_STEPS = 16
X_SLOTS = 3
FF_CHUNK = 1024
W_CHUNK_COLS = 1024
W_CAST_ROWS = 32
VMEM_LIMIT_BYTES = 58 * 1024 * 1024

F32 = jnp.float32
BF16 = jnp.bfloat16


def _dot(a, b):
    return jnp.dot(a, b, preferred_element_type=F32)


def _sigmoid(x):
    return 1.0 / (1.0 + jnp.exp(-x))


def _rmsnorm(x, g):
    y = x * lax.rsqrt(jnp.mean(x * x, axis=-1, keepdims=True) + RMS_EPS)
    return y * g


def _full_spec(shape):
    return pl.BlockSpec(shape, lambda i: (0,) * len(shape))


def _discretize(a_re, a_im, dt):
    mag = jnp.exp(a_re * dt)
    ang = a_im * dt
    lam_re = mag * jnp.cos(ang)
    lam_im = mag * jnp.sin(ang)
    p = lam_re - 1.0
    q = lam_im
    den = a_re * a_re + a_im * a_im
    coef_re = (p * a_re + q * a_im) / den
    coef_im = (q * a_re - p * a_im) / den
    return lam_re, lam_im, coef_re, coef_im


def _tile_states(x):
    x2 = jnp.concatenate([x, x], axis=-1)
    return jnp.concatenate([x2] * (SLAB_STATES // (2 * SSM_STATE)), axis=-1)


def _scale_state_cols(m, lr, li):
    re, im = m[:, :SLAB_STATES], m[:, SLAB_STATES:]
    return jnp.concatenate([lr * re - li * im, lr * im + li * re], axis=-1)


def _scale_state_cols_ct(ct, lr, li):
    re, m = ct[:, :SLAB_STATES], ct[:, SLAB_STATES:]
    return jnp.concatenate([lr * re + li * m, lr * m - li * re], axis=-1)


def _ssm_prep_kernel(a_re_ref, a_im_ref, log_dt_ref, b_re_ref, b_im_ref, c_re_ref, c_im_ref,
                     lam_ref, lam2_ref, bc_ref, cc_ref, d_ref):
    groups, n = a_re_ref.shape
    slabs = bc_ref.shape[0]
    chans = b_re_ref.shape[0] // groups
    on_diag = (lax.broadcasted_iota(jnp.int32, (groups, groups), 0)
               == lax.broadcasted_iota(jnp.int32, (groups, groups), 1))
    log_dt = jnp.sum(jnp.where(on_diag, log_dt_ref[...], 0.0), axis=1, keepdims=True)
    dt = jnp.exp(log_dt)
    a_re = a_re_ref[...]
    a_im = a_im_ref[...]

    lr, li, _, _ = _discretize(_tile_states(a_re), _tile_states(a_im), dt)
    row = lax.broadcasted_iota(jnp.int32, (groups, SLAB_STATES), 0)
    col = lax.broadcasted_iota(jnp.int32, (groups, SLAB_STATES), 1)
    own = (row % GROUPS_PER_SLAB) == (col // n)
    lr = jnp.where(own, lr, 0.0).reshape(slabs, GROUPS_PER_SLAB, SLAB_STATES).sum(axis=1)
    li = jnp.where(own, li, 0.0).reshape(slabs, GROUPS_PER_SLAB, SLAB_STATES).sum(axis=1)
    lam_ref[0] = lr
    lam_ref[1] = li
    lam2_ref[0] = lr * lr - li * li
    lam2_ref[1] = 2.0 * lr * li

    rep = lambda a: jnp.broadcast_to(a[:, None, :], (groups, chans, a.shape[-1])).reshape(
        groups * chans, a.shape[-1])
    lam_re_rows, lam_im_rows, coef_re, coef_im = _discretize(rep(a_re), rep(a_im), rep(dt))
    b_re = b_re_ref[...]
    b_im = b_im_ref[...]
    bbar_re = coef_re * b_re - coef_im * b_im
    bbar_im = coef_re * b_im + coef_im * b_re

    r2 = lax.broadcasted_iota(jnp.int32, (LANES, SLAB_STATES), 0)
    c2 = lax.broadcasted_iota(jnp.int32, (LANES, SLAB_STATES), 1)
    diag = (r2 // chans) == (c2 // n)
    block_diag = lambda m: jnp.where(diag, _tile_states(m), 0.0)
    zero = jnp.zeros((LANES, LANES), F32)
    same_group = (lax.broadcasted_iota(jnp.int32, (LANES, LANES), 0) // chans
                  == lax.broadcasted_iota(jnp.int32, (LANES, LANES), 1) // chans)

    def feed_through(m_re, m_im, rows):
        contract_states = (((1,), (1,)), ((), ()))
        full = (lax.dot_general(m_re, c_re_ref[rows, :], contract_states,
                                precision=lax.Precision.HIGHEST, preferred_element_type=F32)
                - lax.dot_general(m_im, c_im_ref[rows, :], contract_states,
                                  precision=lax.Precision.HIGHEST, preferred_element_type=F32))
        return jnp.where(same_group, full, 0.0)

    for j in range(slabs):
        rows = slice(LANES * j, LANES * (j + 1))
        lrj, lij = lr[j:j + 1, :], li[j:j + 1, :]
        b = jnp.concatenate([block_diag(bbar_re[rows]), block_diag(bbar_im[rows])], axis=-1)
        ct = jnp.concatenate([block_diag(c_re_ref[rows, :]), -block_diag(c_im_ref[rows, :])], axis=-1)
        lam_b = _scale_state_cols(b, lrj, lij)
        c_lam_t = _scale_state_cols_ct(ct, lrj, lij)
        c_lam2_t = _scale_state_cols_ct(c_lam_t, lrj, lij)
        bc_ref[j] = jnp.concatenate([lam_b, b], axis=0).astype(BF16)
        cc_ref[j] = jnp.concatenate([c_lam_t, c_lam2_t], axis=0).T.astype(BF16)
        cb = feed_through(bbar_re[rows], bbar_im[rows], rows)
        lam_r, lam_i = lam_re_rows[rows], lam_im_rows[rows]
        c_lam_b = feed_through(lam_r * bbar_re[rows] - lam_i * bbar_im[rows],
                               lam_r * bbar_im[rows] + lam_i * bbar_re[rows], rows)
        d_ref[j] = jnp.concatenate([jnp.concatenate([cb, c_lam_b], axis=-1),
                                    jnp.concatenate([zero, cb], axis=-1)], axis=0).astype(BF16)


def _pool_hist_kernel(cache_ref, hist_ref):
    for g, w in enumerate(POOL_WINDOWS):
        cols = slice(g * LANES, (g + 1) * LANES)
        acc = cache_ref[POOL_BUF - 1, :, cols]
        for s in range(2, w):
            acc = acc + cache_ref[POOL_BUF - s, :, cols]
        hist_ref[:, cols] = acc


def _pool_project(pooled, w_pool_ref):
    return _dot(jnp.concatenate(pooled, axis=-1).astype(BF16), w_pool_ref[...])


def _pool_fold_kernel(pool_w_ref, pool_scale_ref, pool_out_ref, w_pool_ref):
    for g in range(len(POOL_WINDOWS)):
        rows = slice(g * LANES, (g + 1) * LANES)
        scaled = pool_w_ref[rows, :] * pool_scale_ref[:, rows]
        w_pool_ref[rows, :] = jnp.dot(scaled, pool_out_ref[rows, :], precision=lax.Precision.HIGHEST,
                                      preferred_element_type=F32).astype(BF16)


def _prep_kernel(a_re_ref, a_im_ref, log_dt_ref, b_re_ref, b_im_ref, c_re_ref, c_im_ref,
                 pool_w_ref, pool_scale_ref, pool_out_ref, cache_ref,
                 lam_ref, lam2_ref, bc_ref, cc_ref, d_ref, w_pool_ref, hist_ref):
    _ssm_prep_kernel(a_re_ref, a_im_ref, log_dt_ref, b_re_ref, b_im_ref, c_re_ref, c_im_ref,
                     lam_ref, lam2_ref, bc_ref, cc_ref, d_ref)
    _pool_fold_kernel(pool_w_ref, pool_scale_ref, pool_out_ref, w_pool_ref)
    _pool_hist_kernel(cache_ref, hist_ref)


def _prep(a_re, a_im, log_dt, b_re, b_im, c_re, c_im, pool_w, pool_scale, pool_out, cache_tm):
    g, n = a_re.shape
    c = b_re.shape[-1]
    slabs = g // GROUPS_PER_SLAB
    pool_groups, pool_gw, _ = pool_w.shape
    ins = (a_re, a_im, log_dt.reshape(1, g),
           b_re.transpose(0, 2, 1).reshape(g * c, n), b_im.transpose(0, 2, 1).reshape(g * c, n),
           c_re.reshape(g * c, n), c_im.reshape(g * c, n),
           pool_w.reshape(pool_groups * pool_gw, pool_gw), pool_scale.reshape(1, -1), pool_out,
           cache_tm)
    outs = (jax.ShapeDtypeStruct((2, slabs, SLAB_STATES), F32),
            jax.ShapeDtypeStruct((2, slabs, SLAB_STATES), F32),
            jax.ShapeDtypeStruct((slabs, 2 * LANES, 2 * SLAB_STATES), BF16),
            jax.ShapeDtypeStruct((slabs, 2 * SLAB_STATES, 2 * LANES), BF16),
            jax.ShapeDtypeStruct((slabs, 2 * LANES, 2 * LANES), BF16),
            jax.ShapeDtypeStruct(pool_out.shape, BF16),
            jax.ShapeDtypeStruct(cache_tm.shape[1:], F32))
    *ssm_operands, w_pool, hist = pl.pallas_call(
        _prep_kernel,
        out_shape=outs,
        grid=(1,),
        in_specs=[_full_spec(x.shape) for x in ins],
        out_specs=[_full_spec(o.shape) for o in outs],
        compiler_params=pltpu.CompilerParams(vmem_limit_bytes=VMEM_LIMIT_BYTES),
        name="prep",
    )(*ins)
    return ssm_operands, w_pool, hist


def _finish_layer(x, h1, branch_pool, ys, w_in_ref, glu_ref, w_out_ref, g2_ref,
                  w_up_ref, w_down_ref):
    d = x.shape[-1]
    z = _dot(ys.astype(BF16), glu_ref[...])
    branch_ssm = z[:, :d] * _sigmoid(z[:, d:])
    gate_off = w_in_ref.shape[1] - 2 * d
    g_pool = _sigmoid(_dot(h1, w_in_ref[:, gate_off:gate_off + d]))
    g_ssm = _sigmoid(_dot(h1, w_in_ref[:, gate_off + d:]))
    merged = g_pool * branch_pool + g_ssm * branch_ssm
    x1 = x + _dot(merged.astype(BF16), w_out_ref[...])
    h2 = _rmsnorm(x1, g2_ref[...]).astype(BF16)
    d_ff = w_up_ref.shape[1]
    acc = None
    for c0 in range(0, d_ff, FF_CHUNK):
        a = jnp.maximum(_dot(h2, w_up_ref[:, c0:c0 + FF_CHUNK]), 0.0)
        part = _dot((a * a).astype(BF16), w_down_ref[c0:c0 + FF_CHUNK, :])
        acc = part if acc is None else acc + part
    return x1 + acc


def _load_weights_as_bf16(pairs, slots, sem):
    rows, cols = slots[0].shape
    ring = len(slots)
    chunks = []
    for w_hbm, w_vm in pairs:
        k, n = w_hbm.shape
        for r0 in range(0, k, rows):
            for c0 in range(0, n, cols):
                chunks.append((w_hbm, w_vm, r0, c0))

    def copy(idx):
        w_hbm, _, r0, c0 = chunks[idx]
        sl = idx % ring
        return pltpu.make_async_copy(w_hbm.at[pl.ds(r0, rows), pl.ds(c0, cols)], slots[sl], sem.at[sl])

    for idx in range(min(ring, len(chunks))):
        copy(idx).start()
    for idx, (_, w_vm, r0, c0) in enumerate(chunks):
        slot = slots[idx % ring]
        copy(idx).wait()

        def cast_rows(g, carry, w_vm=w_vm, r0=r0, c0=c0, slot=slot):
            r = pl.multiple_of(g * W_CAST_ROWS, W_CAST_ROWS)
            w_vm[pl.ds(r0 + r, W_CAST_ROWS), c0:c0 + cols] = slot[pl.ds(r, W_CAST_ROWS), :].astype(BF16)
            return carry

        lax.fori_loop(0, rows // W_CAST_ROWS, cast_rows, 0)
        if idx + ring < len(chunks):
            copy(idx + ring).start()


def _layer_kernel(n_tiles,
                  x_hbm, w_in_hbm, glu_hbm, w_out_hbm, w_up_hbm, w_down_hbm,
                  g1_ref, w_pool_ref, lam_ref, lam2_ref, bc_ref, cc_ref, d_ref, d_skip_ref,
                  g2_ref, gf_ref,
                  xs_hbm, hist_ref, h0t_re_ref, h0t_im_ref, cache_hbm,
                  y_hbm, pool_s_hbm, ys_hbm, pool_tail_ref, h_fin_ref, hst_re_ref, hst_im_ref,
                  xbuf, ybuf, sem_in, sem_out, sem_w, sem_pool, sem_dec, h1buf, x2buf, vbuf, s_buf, h_state,
                  vs_buf, xs_buf, ys_buf,
                  w_in_ref, glu_ref, w_out_ref, w_up_ref, w_down_ref):
    n_x, steps, batch, d_model = xbuf.shape
    rows = steps * batch
    halo = HALO_STEPS * batch
    n_slabs = bc_ref.shape[0]
    pool_width = vbuf.shape[1]

    def in_copy(tile, b):
        sl = lax.rem(tile, n_x)
        return pltpu.make_async_copy(x_hbm.at[b, pl.ds(tile * steps, steps), :],
                                     xbuf.at[sl, :, b, :], sem_in.at[sl, b])

    def out_copy(tile, b):
        sl = lax.rem(tile, 2)
        return pltpu.make_async_copy(ybuf.at[sl, :, b, :],
                                     y_hbm.at[b, pl.ds(tile * steps, steps), :], sem_out.at[sl, b])

    def final_norm_to_ybuf(sl):
        y = _rmsnorm(x2buf[sl], gf_ref[...])
        ybuf[sl] = y.reshape(steps, batch, d_model)

    def cache_shift_copy():
        return pltpu.make_async_copy(cache_hbm.at[pl.ds(1, POOL_BUF - 1)],
                                     pool_s_hbm.at[pl.ds(0, POOL_BUF - 1)], sem_pool.at[0])

    def cache_new_row_copy():
        return pltpu.make_async_copy(vs_buf, pool_s_hbm.at[POOL_BUF - 1], sem_pool.at[1])

    def xs_copy():
        return pltpu.make_async_copy(xs_hbm.at[:, 0, :], xs_buf, sem_dec.at[0])

    def ys_copy():
        return pltpu.make_async_copy(ys_buf, ys_hbm.at[:, 0, :], sem_dec.at[1])

    def ssm_cols(j):
        re_cols = slice(2 * SLAB_STATES * j, 2 * SLAB_STATES * j + SLAB_STATES)
        im_cols = slice(2 * SLAB_STATES * j + SLAB_STATES, 2 * SLAB_STATES * (j + 1))
        return slice(SLAB_STATES * j, SLAB_STATES * (j + 1)), re_cols, im_cols

    def prologue():
        for b in range(batch):
            in_copy(0, b).start()
        for b in range(batch):
            in_copy(1, b).start()
        cache_shift_copy().start()
        xs_copy().start()
        stage_slots = ([s_buf.at[:, pl.ds(k * W_CHUNK_COLS, W_CHUNK_COLS)]
                        for k in range(s_buf.shape[1] // W_CHUNK_COLS)]
                       + [x2buf.at[k] for k in range(x2buf.shape[0])])
        _load_weights_as_bf16(
            [(w_in_hbm, w_in_ref), (glu_hbm, glu_ref), (w_out_hbm, w_out_ref),
             (w_up_hbm, w_up_ref), (w_down_hbm, w_down_ref)], stage_slots, sem_w)
        vbuf[0:halo, :] = jnp.zeros((halo, pool_width), F32)
        h_state[...] = jnp.zeros(h_state.shape, F32)
        x2buf[1] = jnp.zeros(x2buf.shape[1:], F32)
        for b in range(batch):
            in_copy(0, b).wait()
        h1buf[0] = _rmsnorm(xbuf[0].reshape(rows, d_model), g1_ref[...]).astype(BF16)

    def prompt_step(i, par):
        xslot = lax.rem(i, n_x)
        xslot_next = lax.rem(i + 1, n_x)

        @pl.when(i + 2 < n_tiles)
        def _():
            for b in range(batch):
                in_copy(i + 2, b).start()

        @pl.when(i + 1 < n_tiles)
        def _():
            for b in range(batch):
                in_copy(i + 1, b).wait()

        @pl.when(i >= 3)
        def _():
            for b in range(batch):
                out_copy(i - 3, b).wait()

        def tile_block():
            final_norm_to_ybuf(1 - par)
            h1buf[1 - par] = _rmsnorm(xbuf[xslot_next].reshape(rows, d_model), g1_ref[...]).astype(BF16)

            x = xbuf[xslot].reshape(rows, d_model)
            h1 = h1buf[par]
            v = _dot(h1, w_in_ref[:, 0:pool_width])
            vbuf[halo:halo + rows, :] = v
            u = _dot(h1, w_in_ref[:, pool_width:2 * pool_width])

            row = lax.broadcasted_iota(jnp.int32, (rows, LANES), 0)
            pos = i * steps + lax.shift_right_logical(row, int(math.log2(batch))) + 1
            pooled = []
            for g, w in enumerate(POOL_WINDOWS):
                cols = slice(g * LANES, (g + 1) * LANES)
                cur = vbuf[halo:halo + rows, cols]
                win = cur
                for s in range(1, w):
                    win = win + vbuf[halo - s * batch:halo - s * batch + rows, cols]
                cnt = jnp.minimum(w, pos).astype(F32)
                pooled.append(win / cnt - cur)
            branch_pool = _pool_project(pooled, w_pool_ref)

            pairs = steps // 2
            half = pairs * batch
            ssm_width = u.shape[1]
            u4 = u.reshape(pairs, 2, batch, ssm_width)
            u_even = u4[:, 0].reshape(half, ssm_width)
            u_odd = u4[:, 1].reshape(half, ssm_width)
            u_pair = [jnp.concatenate([u_even[:, LANES * j:LANES * (j + 1)],
                                       u_odd[:, LANES * j:LANES * (j + 1)]], axis=-1).astype(BF16)
                      for j in range(n_slabs)]
            for j in range(n_slabs):
                s_buf[0:half, 2 * SLAB_STATES * j:2 * SLAB_STATES * (j + 1)] = _dot(u_pair[j], bc_ref[j])
            for j in range(n_slabs):
                _, re_cols, im_cols = ssm_cols(j)
                lr2 = jnp.broadcast_to(lam2_ref[0, j:j + 1, :], (batch, SLAB_STATES))
                li2 = jnp.broadcast_to(lam2_ref[1, j:j + 1, :], (batch, SLAB_STATES))
                hr = h_state[:, re_cols]
                hi = h_state[:, im_cols]
                for p in range(pairs):
                    r = slice(p * batch, (p + 1) * batch)
                    sr = s_buf[r, re_cols]
                    si = s_buf[r, im_cols]
                    s_buf[r, re_cols] = hr
                    s_buf[r, im_cols] = hi
                    hr, hi = lr2 * hr - li2 * hi + sr, lr2 * hi + li2 * hr + si
                h_state[:, re_cols] = hr
                h_state[:, im_cols] = hi
            y_pair = [_dot(s_buf[0:half, 2 * SLAB_STATES * j:2 * SLAB_STATES * (j + 1)].astype(BF16),
                           cc_ref[j]) + _dot(u_pair[j], d_ref[j]) for j in range(n_slabs)]
            y_even = jnp.concatenate([y[:, :LANES] for y in y_pair], axis=-1)
            y_odd = jnp.concatenate([y[:, LANES:] for y in y_pair], axis=-1)
            ys = jnp.concatenate([y_even.reshape(pairs, 1, batch, ssm_width),
                                  y_odd.reshape(pairs, 1, batch, ssm_width)], axis=1).reshape(rows, ssm_width)
            ys = ys + d_skip_ref[...] * u

            x2buf[par] = _finish_layer(x, h1, branch_pool, ys, w_in_ref, glu_ref, w_out_ref, g2_ref,
                                       w_up_ref, w_down_ref)

            tail = vbuf[rows:rows + halo, :]
            vbuf[0:halo, :] = tail
            pool_tail_ref[...] = tail[halo - POOL_BUF * batch:, :]
            h_fin_ref[...] = h_state[...]

        tile_block()

        @pl.when(i >= 1)
        def _():
            for b in range(batch):
                out_copy(i - 1, b).start()

    def epilogue():
        last = n_tiles - 1
        for b in range(batch):
            out_copy(last - 2, b).wait()
        final_norm_to_ybuf(last % 2)
        for b in range(batch):
            out_copy(last, b).start()
        for b in range(batch):
            out_copy(last - 1, b).wait()
        for b in range(batch):
            out_copy(last, b).wait()

    def decode_step():
        xs_copy().wait()
        x = xs_buf[...]
        h1 = _rmsnorm(x, g1_ref[...]).astype(BF16)
        v = _dot(h1, w_in_ref[:, 0:pool_width])
        vs_buf[...] = v
        cache_new_row_copy().start()
        u = _dot(h1, w_in_ref[:, pool_width:2 * pool_width])

        pooled = []
        for g, w in enumerate(POOL_WINDOWS):
            cols = slice(g * LANES, (g + 1) * LANES)
            cur = v[:, cols]
            cnt = float(min(w, PAST_LEN + 1))
            pooled.append((hist_ref[:, cols] + cur) / cnt - cur)
        branch_pool = _pool_project(pooled, w_pool_ref)

        u_bf = u.astype(BF16)
        ys_parts = []
        for j in range(n_slabs):
            st, _, _ = ssm_cols(j)
            u_j = u_bf[:, LANES * j:LANES * (j + 1)]
            bu = _dot(u_j, bc_ref[j, LANES:2 * LANES, :])
            lr = lam_ref[0, j:j + 1, :]
            li = lam_ref[1, j:j + 1, :]
            hr = h0t_re_ref[st, :].T
            hi = h0t_im_ref[st, :].T
            nr = lr * hr - li * hi + bu[:, :SLAB_STATES]
            ni = lr * hi + li * hr + bu[:, SLAB_STATES:]
            hst_re_ref[st, :] = nr.T
            hst_im_ref[st, :] = ni.T
            h_prev = jnp.concatenate([hr, hi], axis=-1).astype(BF16)
            ys_parts.append(_dot(h_prev, cc_ref[j, :, 0:LANES]) + _dot(u_j, d_ref[j, 0:LANES, 0:LANES]))
        ys = jnp.concatenate(ys_parts, axis=-1) + d_skip_ref[...] * u

        x2 = _finish_layer(x, h1, branch_pool, ys, w_in_ref, glu_ref, w_out_ref, g2_ref,
                           w_up_ref, w_down_ref)
        ys_buf[...] = _rmsnorm(x2, gf_ref[...])
        ys_copy().start()
        cache_shift_copy().wait()
        cache_new_row_copy().wait()
        ys_copy().wait()

    def tile_pair(k, carry):
        prompt_step(2 * k, 0)
        prompt_step(2 * k + 1, 1)
        return carry

    prologue()
    lax.fori_loop(0, n_tiles // 2, tile_pair, 0)
    epilogue()
    decode_step()


def _const_spec(shape):
    return pl.BlockSpec(shape, lambda i: (0,) * len(shape), pipeline_mode=pl.Buffered(1))


def _layer_call(x, big_weights, small_params, sample_ins):
    batch, seq, d = x.shape
    steps = STEPS_PER_TILE
    rows = steps * batch
    n_tiles = seq // steps
    assert n_tiles >= X_SLOTS and n_tiles % 2 == 0, "input ring of 3 tiles; tiles are walked in pairs"
    xs, hist, h0t_re, _, cache_tm = sample_ins
    dec_rows = xs.shape[0]
    pool_width = hist.shape[-1]
    n_states = h0t_re.shape[0]
    halo = HALO_STEPS * batch
    assert d == W_CHUNK_COLS and all(w.shape[0] % rows == 0 and w.shape[1] % W_CHUNK_COLS == 0
                                     for w in big_weights), "weight staging chunks must tile every weight"
    out_shape = (jax.ShapeDtypeStruct((batch, seq, d), F32),
                 jax.ShapeDtypeStruct(cache_tm.shape, F32),
                 jax.ShapeDtypeStruct(xs.shape, F32),
                 jax.ShapeDtypeStruct((POOL_BUF * batch, pool_width), F32),
                 jax.ShapeDtypeStruct((batch, 2 * n_states), F32),
                 jax.ShapeDtypeStruct((n_states, dec_rows), F32),
                 jax.ShapeDtypeStruct((n_states, dec_rows), F32))
    any_spec = pl.BlockSpec(memory_space=pl.ANY)
    return pl.pallas_call(
        functools.partial(_layer_kernel, n_tiles),
        out_shape=out_shape,
        grid=(1,),
        in_specs=([any_spec] * (1 + len(big_weights))
                  + [_const_spec(p.shape) for p in small_params]
                  + [any_spec] + [_const_spec(a.shape) for a in sample_ins[1:-1]] + [any_spec]),
        out_specs=[any_spec] * 3 + [_full_spec(o.shape) for o in out_shape[3:]],
        scratch_shapes=[pltpu.VMEM((X_SLOTS, steps, batch, d), F32),
                        pltpu.VMEM((2, steps, batch, d), F32),
                        pltpu.SemaphoreType.DMA((X_SLOTS, batch)),
                        pltpu.SemaphoreType.DMA((2, batch)),
                        pltpu.SemaphoreType.DMA((2 * n_states // W_CHUNK_COLS + 2,)),
                        pltpu.SemaphoreType.DMA((2,)),
                        pltpu.SemaphoreType.DMA((2,)),
                        pltpu.VMEM((2, rows, d), BF16),
                        pltpu.VMEM((2, rows, d), F32),
                        pltpu.VMEM((halo + rows, pool_width), F32),
                        pltpu.VMEM((rows, 2 * n_states), F32),
                        pltpu.VMEM((batch, 2 * n_states), F32),
                        pltpu.VMEM((dec_rows, pool_width), F32),
                        pltpu.VMEM((dec_rows, d), F32),
                        pltpu.VMEM((dec_rows, d), F32)]
                       + [pltpu.VMEM(w.shape, BF16) for w in big_weights],
        compiler_params=pltpu.CompilerParams(dimension_semantics=("arbitrary",),
                                             vmem_limit_bytes=VMEM_LIMIT_BYTES),
        name="layer",
    )(x, *big_weights, *small_params, *sample_ins)


def kernel(x_prompt, x_sample, cache_pool, state_ssm_re, state_ssm_im, norm1_g, w_in, pool_w,
           pool_scale, pool_out, ssm_a_re, ssm_a_im, ssm_log_dt, ssm_b_re, ssm_b_im, ssm_c_re,
           ssm_c_im, ssm_d, ssm_glu, w_out, norm2_g, w_up, w_down, normf_g):
    depth = w_in.shape[0]
    assert depth == 1, "the final norm is fused into the (single) layer kernel"
    batch, seq, d = x_prompt.shape
    dec_batch, dec_seq, _ = x_sample.shape
    assert dec_seq == 1 and batch == SUBLANES and seq % STEPS_PER_TILE == 0
    groups, n_state = ssm_a_re.shape[1:]
    n_states = groups * n_state
    l = 0

    cache_tm = cache_pool[l].transpose(1, 0, 2)
    state_t = lambda s: s.transpose(1, 2, 0).reshape(n_states, dec_batch)
    ssm_operands, w_pool, hist = _prep(ssm_a_re[l], ssm_a_im[l], ssm_log_dt[l], ssm_b_re[l],
                                       ssm_b_im[l], ssm_c_re[l], ssm_c_im[l],
                                       pool_w[l], pool_scale[l], pool_out[l], cache_tm)
    row = lambda a: a.reshape(1, -1)
    big_weights = (w_in[l], ssm_glu[l], w_out[l], w_up[l], w_down[l])
    small_params = (row(norm1_g[l]), w_pool, *ssm_operands,
                    row(ssm_d[l]), row(norm2_g[l]), row(normf_g))
    sample_ins = (x_sample, hist, state_t(state_ssm_re[l]), state_t(state_ssm_im[l]), cache_tm)

    y_prompt, pool_s_tm, y_s, pool_tail, h_fin, re_s_t, im_s_t = _layer_call(
        x_prompt, big_weights, small_params, sample_ins)

    pool_width = pool_tail.shape[-1]
    pool_p = pool_tail.reshape(POOL_BUF, batch, pool_width).transpose(1, 0, 2)
    h_fin = h_fin.reshape(batch, n_states // SLAB_STATES, 2, SLAB_STATES)
    re_p = h_fin[:, :, 0].reshape(batch, groups, n_state)
    im_p = h_fin[:, :, 1].reshape(batch, groups, n_state)
    pool_s = pool_s_tm.transpose(1, 0, 2)
    state_back = lambda s: s.reshape(groups, n_state, dec_batch).transpose(2, 0, 1)

    return (y_prompt, y_s,
            pool_p[None], re_p[None], im_p[None],
            pool_s[None], state_back(re_s_t)[None], state_back(im_s_t)[None])
```
